```python
import math, functools
import jax, jax.numpy as jnp
from jax import lax
import numpy as np

D_MODEL = 1024
BATCH = 4
SEQ = 4096
DEPTH = 2
DEC_BATCH = 128
DEC_SEQ = 1
PAST_LEN = 2048
PAGE_SIZE = 128

D_HEAD = 64
V_HEAD = 2 * D_HEAD
H_ATT = D_MODEL // V_HEAD
ROT_DIM = D_HEAD // 4
ROPE_THETA = 500000.0
Q_WIDTH = 2 * H_ATT * D_HEAD
K_WIDTH = 2 * H_ATT * D_HEAD
V_WIDTH = H_ATT * V_HEAD
Q_BLOCK = 128
CONV_CH = D_MODEL
CONV_K = 31
FFN_DIM = 2816
FFN_CONV_K = 3
IN_WIDTH = Q_WIDTH + K_WIDTH + V_WIDTH + 2 * CONV_CH + 2 * D_MODEL
ALPHA = (2 * DEPTH) ** 0.25
BETA = (8 * DEPTH) ** -0.25
LN_EPS = 1e-5

kernel_name = 'hybrid_conformer_diffattn_decoder_step'


def layer_norm(x, g, b):
    xf = x.astype(jnp.float32)
    mu = jnp.mean(xf, -1, keepdims=True)
    var = jnp.mean(jnp.square(xf - mu), -1, keepdims=True)
    return ((xf - mu) * lax.rsqrt(var + LN_EPS) * g + b).astype(x.dtype)


def rms_norm(x, g):
    xf = x.astype(jnp.float32)
    return (xf * lax.rsqrt(jnp.mean(xf * xf, -1, keepdims=True) + LN_EPS) * g).astype(x.dtype)


def rope_partial(x, pos):
    half = ROT_DIM // 2
    inv = ROPE_THETA ** (-jnp.arange(half, dtype=jnp.float32) / half)
    ang = pos.astype(jnp.float32)[:, None] * inv[None, :]
    cos = jnp.cos(ang)[:, None, :]
    sin = jnp.sin(ang)[:, None, :]
    xr = x[..., :ROT_DIM].astype(jnp.float32)
    x1, x2 = xr[..., :half], xr[..., half:]
    rot = jnp.concatenate([x1 * cos - x2 * sin, x2 * cos + x1 * sin], -1).astype(x.dtype)
    return jnp.concatenate([rot, x[..., ROT_DIM:]], -1)


def depthwise_conv(x_ext, w, b):
    out = lax.conv_general_dilated(x_ext, w[:, None, :], window_strides=(1,), padding='VALID',
                                   dimension_numbers=('NWC', 'WIO', 'NWC'),
                                   feature_group_count=x_ext.shape[-1])
    return out + b


def diff_attend(q, k, v, mask, lam):
    bn, tq = q.shape[0], q.shape[1]
    tk = k.shape[1]
    s = jnp.einsum('bqhd,bkhd->bhqk', q, k, preferred_element_type=jnp.float32) * (D_HEAD ** -0.5)
    s = jnp.where(mask[None, None], s, -jnp.inf)
    p = jax.nn.softmax(s, axis=-1).reshape(bn, H_ATT, 2, tq, tk)
    a = p[:, :, 0] - lam * p[:, :, 1]
    return jnp.einsum('bhqk,bkhe->bqhe', a.astype(v.dtype), v)


def prompt_attend(q, k, v, lam):
    bn, s_len = q.shape[0], q.shape[1]
    nb = s_len // Q_BLOCK
    qb = jnp.moveaxis(q.reshape(bn, nb, Q_BLOCK, 2 * H_ATT, D_HEAD), 1, 0)
    kpos = jnp.arange(s_len)

    def one_block(args):
        qi, start = args
        mask = (start + jnp.arange(Q_BLOCK))[:, None] >= kpos[None, :]
        return diff_attend(qi, k, v, mask, lam)

    o = lax.map(one_block, (qb, jnp.arange(nb) * Q_BLOCK))
    return jnp.moveaxis(o, 0, 1).reshape(bn, s_len, H_ATT, V_HEAD)


def paged_attend(q, k, v, lam, k_past, v_past):
    p_len, t = k_past.shape[1], q.shape[1]
    k_all = jnp.concatenate([k_past, k], axis=1)
    v_all = jnp.concatenate([v_past, v], axis=1)
    mask = jnp.arange(p_len + t)[None, :] <= (p_len + jnp.arange(t))[:, None]
    return diff_attend(q, k_all, v_all, mask, lam)


def trunk_layer(x, c, pos, conv_hist, ffn_hist, attend, layer,
                w_ada, b_ada, w_in, lambda_qk, attn_subln_g, glu_b, dw_w, dw_b,
                conv_ln_g, conv_ln_b, w_pw2, b_pw2, w_o, ln1_g, ln1_b,
                w_ffn_in, ffn_dw_w, ffn_dw_b, w_down, ln2_g, ln2_b):
    bn, t, _ = x.shape
    ada = jnp.dot(jax.nn.silu(c), w_ada) + b_ada
    sh1, sc1, g1, sh2, sc2, g2 = [a[:, None, :] for a in jnp.split(ada, 6, axis=-1)]

    h = x * (1 + sc1) + sh1
    proj = jnp.einsum('btd,de->bte', h, w_in)
    q, k, v, glu_in, gates = jnp.split(
        proj, [Q_WIDTH, Q_WIDTH + K_WIDTH, Q_WIDTH + K_WIDTH + V_WIDTH,
               Q_WIDTH + K_WIDTH + V_WIDTH + 2 * CONV_CH], axis=-1)

    q = rope_partial(q.reshape(bn, t, 2 * H_ATT, D_HEAD), pos)
    k = rope_partial(k.reshape(bn, t, 2 * H_ATT, D_HEAD), pos)
    v = v.reshape(bn, t, H_ATT, V_HEAD)
    lam_init = 0.8 - 0.6 * math.exp(-0.3 * layer)
    lq = lambda_qk.astype(jnp.float32)
    lam = jnp.exp(jnp.sum(lq[0] * lq[1])) - jnp.exp(jnp.sum(lq[2] * lq[3])) + lam_init
    o = attend(q, k, v, lam)
    attn_out = (rms_norm(o, attn_subln_g) * (1 - lam_init)).reshape(bn, t, V_WIDTH)

    u = glu_in + glu_b
    u = u[..., :CONV_CH] * jax.nn.sigmoid(u[..., CONV_CH:])
    u_ext = jnp.concatenate([conv_hist, u], axis=1)
    new_conv_hist = u_ext[:, -(CONV_K - 1):]
    cv = jax.nn.silu(layer_norm(depthwise_conv(u_ext, dw_w, dw_b), conv_ln_g, conv_ln_b))
    conv_out = jnp.einsum('btc,cd->btd', cv, w_pw2) + b_pw2

    gate_a, gate_b = jnp.split(jax.nn.sigmoid(gates), 2, axis=-1)
    merged = gate_a * conv_out + gate_b * attn_out
    mix = jnp.einsum('btd,de->bte', merged, w_o)
    x = layer_norm(ALPHA * x + g1 * mix, ln1_g, ln1_b)

    h = x * (1 + sc2) + sh2
    gt, up = jnp.split(jnp.einsum('btd,df->btf', h, w_ffn_in), 2, axis=-1)
    g_ext = jnp.concatenate([ffn_hist, gt], axis=1)
    new_ffn_hist = g_ext[:, -(FFN_CONV_K - 1):]
    gt = depthwise_conv(g_ext, ffn_dw_w, ffn_dw_b)
    f = jnp.einsum('btf,fd->btd', jax.nn.silu(gt) * up, w_down)
    x = layer_norm(ALPHA * x + g2 * f, ln2_g, ln2_b)
    return x, k, v, new_conv_hist, new_ffn_hist


def setup_inputs(seed: int = 0) -> dict:
    key = jax.random.key(seed)
    ks = jax.random.split(key, 40)
    n_pages = PAST_LEN // PAGE_SIZE
    n_used = DEC_BATCH * n_pages
    n_phys = n_used + max(n_used // 4, 1)

    def nrm(k, shape, scale):
        return jax.random.normal(k, shape, jnp.float32) * scale

    page_table = jax.random.permutation(ks[0], n_phys)[:n_used].reshape(DEC_BATCH, n_pages).astype(jnp.int32)
    cols = jnp.arange(IN_WIDTH)
    v_cols = (cols >= Q_WIDTH + K_WIDTH) & (cols < Q_WIDTH + K_WIDTH + V_WIDTH)
    w_in = nrm(ks[8], (DEPTH, D_MODEL, IN_WIDTH), D_MODEL ** -0.5) * jnp.where(v_cols, BETA, 1.0)
    return {
        'x_prompt': nrm(ks[1], (BATCH, SEQ, D_MODEL), 1.0),
        'x_sample': nrm(ks[2], (DEC_BATCH, DEC_SEQ, D_MODEL), 1.0),
        'cache_k': nrm(ks[3], (DEPTH, n_phys, PAGE_SIZE, 2 * H_ATT, D_HEAD), 1.0),
        'cache_v': nrm(ks[4], (DEPTH, n_phys, PAGE_SIZE, H_ATT, V_HEAD), BETA),
        'state_conv': nrm(ks[5], (DEPTH, DEC_BATCH, CONV_K - 1, CONV_CH), 0.5),
        'state_ffn_conv': nrm(ks[6], (DEPTH, DEC_BATCH, FFN_CONV_K - 1, FFN_DIM), 1.0),
        'page_table': page_table,
        'c_prompt': nrm(ks[7], (BATCH, D_MODEL), 1.0),
        'c_sample': nrm(ks[9], (DEC_BATCH, D_MODEL), 1.0),
        'w_ada': nrm(ks[10], (DEPTH, D_MODEL, 6 * D_MODEL), D_MODEL ** -0.5),
        'b_ada': nrm(ks[11], (DEPTH, 6 * D_MODEL), 0.01),
        'w_in': w_in,
        'lambda_qk': nrm(ks[12], (DEPTH, 4, D_HEAD), 0.1),
        'attn_subln_g': 1.0 + nrm(ks[13], (DEPTH, V_HEAD), 0.1),
        'glu_b': nrm(ks[14], (DEPTH, 2 * CONV_CH), 0.01),
        'dw_w': nrm(ks[15], (DEPTH, CONV_K, CONV_CH), CONV_K ** -0.5),
        'dw_b': nrm(ks[16], (DEPTH, CONV_CH), 0.01),
        'conv_ln_g': 1.0 + nrm(ks[17], (DEPTH, CONV_CH), 0.1),
        'conv_ln_b': nrm(ks[18], (DEPTH, CONV_CH), 0.01),
        'w_pw2': nrm(ks[19], (DEPTH, CONV_CH, D_MODEL), BETA * CONV_CH ** -0.5),
        'b_pw2': nrm(ks[20], (DEPTH, D_MODEL), 0.01),
        'w_o': nrm(ks[21], (DEPTH, D_MODEL, D_MODEL), BETA * D_MODEL ** -0.5),
        'ln1_g': 1.0 + nrm(ks[22], (DEPTH, D_MODEL), 0.1),
        'ln1_b': nrm(ks[23], (DEPTH, D_MODEL), 0.01),
        'w_ffn_in': nrm(ks[24], (DEPTH, D_MODEL, 2 * FFN_DIM), D_MODEL ** -0.5),
        'ffn_dw_w': nrm(ks[25], (DEPTH, FFN_CONV_K, FFN_DIM), FFN_CONV_K ** -0.5),
        'ffn_dw_b': nrm(ks[26], (DEPTH, FFN_DIM), 0.01),
        'w_down': nrm(ks[27], (DEPTH, FFN_DIM, D_MODEL), BETA * FFN_DIM ** -0.5),
        'ln2_g': 1.0 + nrm(ks[28], (DEPTH, D_MODEL), 0.1),
        'ln2_b': nrm(ks[29], (DEPTH, D_MODEL), 0.01),
    }


def reference(x_prompt, x_sample, cache_k, cache_v, state_conv, state_ffn_conv, page_table,
              c_prompt, c_sample, w_ada, b_ada, w_in, lambda_qk, attn_subln_g, glu_b, dw_w, dw_b,
              conv_ln_g, conv_ln_b, w_pw2, b_pw2, w_o, ln1_g, ln1_b, w_ffn_in, ffn_dw_w, ffn_dw_b,
              w_down, ln2_g, ln2_b):
    bp, seq = x_prompt.shape[0], x_prompt.shape[1]
    bs, dec_seq = x_sample.shape[0], x_sample.shape[1]
    past = page_table.shape[1] * PAGE_SIZE
    pos_prompt = jnp.arange(seq)
    pos_sample = past + jnp.arange(dec_seq)
    xp, xs = x_prompt, x_sample
    k_p, v_p, conv_p, ffn_p = [], [], [], []
    k_s, v_s, conv_s, ffn_s = [], [], [], []
    for l in range(DEPTH):
        lp = (w_ada[l], b_ada[l], w_in[l], lambda_qk[l], attn_subln_g[l], glu_b[l], dw_w[l], dw_b[l],
              conv_ln_g[l], conv_ln_b[l], w_pw2[l], b_pw2[l], w_o[l], ln1_g[l], ln1_b[l],
              w_ffn_in[l], ffn_dw_w[l], ffn_dw_b[l], w_down[l], ln2_g[l], ln2_b[l])
        conv0 = jnp.zeros((bp, CONV_K - 1, CONV_CH), xp.dtype)
        ffn0 = jnp.zeros((bp, FFN_CONV_K - 1, FFN_DIM), xp.dtype)
        xp, kl, vl, ch, fh = trunk_layer(xp, c_prompt, pos_prompt, conv0, ffn0, prompt_attend, l, *lp)
        k_p.append(kl); v_p.append(vl); conv_p.append(ch); ffn_p.append(fh)
        k_past = cache_k[l, page_table].reshape(bs, past, 2 * H_ATT, D_HEAD)
        v_past = cache_v[l, page_table].reshape(bs, past, H_ATT, V_HEAD)
        attend_s = functools.partial(paged_attend, k_past=k_past, v_past=v_past)
        xs, kl, vl, ch, fh = trunk_layer(xs, c_sample, pos_sample, state_conv[l], state_ffn_conv[l],
                                         attend_s, l, *lp)
        k_s.append(kl); v_s.append(vl); conv_s.append(ch); ffn_s.append(fh)
    return (xp, xs,
            jnp.stack(k_p), jnp.stack(v_p), jnp.stack(conv_p), jnp.stack(ffn_p),
            jnp.stack(k_s), jnp.stack(v_s), jnp.stack(conv_s), jnp.stack(ffn_s))
```

```python
import functools
import math

import jax
import jax.numpy as jnp
from jax import lax
from jax.experimental import pallas as pl
from jax.experimental.pallas import tpu as pltpu

BF16 = jnp.bfloat16
F32 = jnp.float32

D_HEAD = 64
V_HEAD = 2 * D_HEAD
ROT_DIM = D_HEAD // 4
ROPE_THETA = 500000.0
CONV_K = 31
FFN_CONV_K = 3
LN_EPS = 1e-5
LANES = 128
NEG_BIG = -1e30
LOG2E = 1.4426950408889634
VMEM_LIMIT_BYTES = 56 * 1024 * 1024

TM_PROMPT = 1024
TN_PROJ = 1024
TM_FFN_DOWN = 512
TQ_ATTN = 256
TK_ATTN = 512
CONV_ROWS = 64
PAGES_PER_STEP = 8


def _params(*sem):
    return pltpu.CompilerParams(dimension_semantics=sem, vmem_limit_bytes=VMEM_LIMIT_BYTES)


def _silu(z):
    return z * jax.nn.sigmoid(z)


def _layer_norm(y, g, b):
    mu = jnp.mean(y, axis=-1, keepdims=True)
    d = y - mu
    var = jnp.mean(d * d, axis=-1, keepdims=True)
    return d * lax.rsqrt(var + LN_EPS) * g + b


def _dot(a, b):
    return jnp.dot(a, b, preferred_element_type=F32)


def _ada_kernel(c_ref, w_ref, b_ref, o_ref):
    s = _silu(c_ref[...]).astype(BF16)
    o_ref[...] = _dot(s, w_ref[...].astype(BF16)) + b_ref[...]


def _ada(c_all, w_ada, b_ada):
    nl, d, n = w_ada.shape
    mc = c_all.shape[0]
    tn = 1536 if n % 1536 == 0 else n
    return pl.pallas_call(
        _ada_kernel,
        grid=(nl, n // tn),
        in_specs=[pl.BlockSpec((mc, d), lambda l, j: (0, 0)),
                  pl.BlockSpec((None, d, tn), lambda l, j: (l, 0, j)),
                  pl.BlockSpec((None, 1, tn), lambda l, j: (l, 0, j))],
        out_specs=pl.BlockSpec((None, mc, tn), lambda l, j: (l, 0, j)),
        out_shape=jax.ShapeDtypeStruct((nl, mc, n), F32),
        compiler_params=_params("arbitrary", "arbitrary"),
        name="ada",
    )(c_all, w_ada, b_ada.reshape(nl, 1, n))


class _Rows:
    def __init__(self, m, tm, seq):
        self.m, self.tm, self.seq = m, tm, seq
        self.nt = m // tm
        if seq >= tm:
            assert seq % tm == 0
            self.tpb, self.mod_rows = seq // tm, 1
        else:
            assert seq == 1
            self.tpb, self.mod_rows = 1, tm

    def mod(self, a):
        return a.reshape(-1, self.mod_rows, a.shape[-1])

    def mod_spec(self, d, grid_rank):
        if grid_rank == 1:
            return pl.BlockSpec((None, self.mod_rows, d), lambda i: (i // self.tpb, 0, 0))
        return pl.BlockSpec((None, self.mod_rows, d), lambda j, i: (i // self.tpb, 0, 0))


def _modulate_kernel(x_ref, sc_ref, sh_ref, h_ref):
    h_ref[...] = (x_ref[...] * (1.0 + sc_ref[...]) + sh_ref[...]).astype(BF16)


def _modulate(rows, x, sc, sh):
    d = x.shape[1]
    row_spec = pl.BlockSpec((rows.tm, d), lambda i: (i, 0))
    return pl.pallas_call(
        _modulate_kernel,
        grid=(rows.nt,),
        in_specs=[row_spec, rows.mod_spec(d, 1), rows.mod_spec(d, 1)],
        out_specs=row_spec,
        out_shape=jax.ShapeDtypeStruct(x.shape, BF16),
        compiler_params=_params("arbitrary"),
        name="modulate",
    )(x, sc, sh)


def _load_weight(w_ref, wbf_ref):
    @pl.when(pl.program_id(1) == 0)
    def _():
        wbf_ref[...] = w_ref[...].astype(BF16)


def _rope_chunk(y, c, sa, sb):
    return y * c + pltpu.roll(y, LANES - ROT_DIM // 2, 1) * sa + pltpu.roll(y, ROT_DIM // 2, 1) * sb


def _q_kernel(h_ref, w_ref, c_ref, sa_ref, sb_ref, q_ref, wbf_ref, *, scale):
    _load_weight(w_ref, wbf_ref)
    y = _dot(h_ref[...], wbf_ref[...])
    c, sa, sb = c_ref[...], sa_ref[...], sb_ref[...]
    for j in range(y.shape[1] // LANES):
        ls = slice(j * LANES, (j + 1) * LANES)
        q_ref[:, ls] = (_rope_chunk(y[:, ls], c, sa, sb) * scale).astype(BF16)


def _kv_kernel(h_ref, w_ref, *rest, rope):
    if rope:
        c_ref, sa_ref, sb_ref, o_ref, obf_ref, wbf_ref = rest
    else:
        o_ref, obf_ref, wbf_ref = rest
    _load_weight(w_ref, wbf_ref)
    y = _dot(h_ref[...], wbf_ref[...])
    if rope:
        c, sa, sb = c_ref[...], sa_ref[...], sb_ref[...]
        for j in range(y.shape[1] // LANES):
            ls = slice(j * LANES, (j + 1) * LANES)
            r = _rope_chunk(y[:, ls], c, sa, sb)
            o_ref[:, ls] = r
            obf_ref[:, ls] = r.astype(BF16)
    else:
        o_ref[...] = y
        obf_ref[...] = y.astype(BF16)


def _glu_kernel(h_ref, wa_ref, wb_ref, ba_ref, bb_ref, u_ref, *rest, tail):
    if tail:
        tail_ref, wabf_ref, wbbf_ref = rest
    else:
        wabf_ref, wbbf_ref = rest
    _load_weight(wa_ref, wabf_ref)
    _load_weight(wb_ref, wbbf_ref)
    h = h_ref[...]
    a = _dot(h, wabf_ref[...]) + ba_ref[...]
    b = _dot(h, wbbf_ref[...]) + bb_ref[...]
    u = a * jax.nn.sigmoid(b)
    u_ref[...] = u
    if tail:
        tail_ref[...] = u[u.shape[0] - tail:, :]


def _gate_kernel(h_ref, w_ref, o_ref, wbf_ref):
    _load_weight(w_ref, wbf_ref)
    o_ref[...] = jax.nn.sigmoid(_dot(h_ref[...], wbf_ref[...])).astype(BF16)


def _proj_specs(rows, d, tn, col0, layer):
    h_spec = pl.BlockSpec((rows.tm, d), lambda j, i: (i, 0))
    w_spec = pl.BlockSpec((None, d, tn), lambda j, i: (layer, 0, col0 // tn + j))
    o_spec = pl.BlockSpec((rows.tm, tn), lambda j, i: (i, j))
    return h_spec, w_spec, o_spec


def _table_specs(rows, tables):
    nt = tables[0].shape[0] // rows.tm
    return [pl.BlockSpec((rows.tm, LANES), lambda j, i: (i % nt, 0))] * 3


def _proj_q(rows, h, w_in, layer, tables, width, scale):
    d = h.shape[1]
    tn = min(TN_PROJ, width)
    h_spec, w_spec, o_spec = _proj_specs(rows, d, tn, 0, layer)
    return pl.pallas_call(
        functools.partial(_q_kernel, scale=scale),
        grid=(width // tn, rows.nt),
        in_specs=[h_spec, w_spec] + _table_specs(rows, tables),
        out_specs=o_spec,
        out_shape=jax.ShapeDtypeStruct((rows.m, width), BF16),
        scratch_shapes=[pltpu.VMEM((d, tn), BF16)],
        compiler_params=_params("arbitrary", "arbitrary"),
        name="proj_q",
    )(h, w_in, *tables)


def _proj_kv(rows, h, w_in, layer, col0, width, tables):
    d = h.shape[1]
    tn = min(TN_PROJ, width)
    h_spec, w_spec, o_spec = _proj_specs(rows, d, tn, col0, layer)
    rope = tables is not None
    extra = _table_specs(rows, tables) if rope else []
    args = tuple(tables) if rope else ()
    return pl.pallas_call(
        functools.partial(_kv_kernel, rope=rope),
        grid=(width // tn, rows.nt),
        in_specs=[h_spec, w_spec] + extra,
        out_specs=[o_spec, o_spec],
        out_shape=[jax.ShapeDtypeStruct((rows.m, width), F32),
                   jax.ShapeDtypeStruct((rows.m, width), BF16)],
        scratch_shapes=[pltpu.VMEM((d, tn), BF16)],
        compiler_params=_params("arbitrary", "arbitrary"),
        name="proj_k" if rope else "proj_v",
    )(h, w_in, *args)


def _proj_glu(rows, h, w_in, glu_b, layer, col0, width, tail):
    d = h.shape[1]
    tn = min(TN_PROJ, width)
    nl = glu_b.shape[0]
    h_spec, wa_spec, o_spec = _proj_specs(rows, d, tn, col0, layer)
    _, wb_spec, _ = _proj_specs(rows, d, tn, col0 + width, layer)
    ba_spec = pl.BlockSpec((None, 1, tn), lambda j, i: (layer, 0, j))
    bb_spec = pl.BlockSpec((None, 1, tn), lambda j, i: (layer, 0, width // tn + j))
    out_specs = [o_spec]
    out_shape = [jax.ShapeDtypeStruct((rows.m, width), F32)]
    if tail:
        out_specs.append(pl.BlockSpec((None, tail, tn), lambda j, i: (i // rows.tpb, 0, j)))
        out_shape.append(jax.ShapeDtypeStruct((rows.m // rows.seq, tail, width), F32))
    gb = glu_b.reshape(nl, 1, 2 * width)
    return pl.pallas_call(
        functools.partial(_glu_kernel, tail=tail),
        grid=(width // tn, rows.nt),
        in_specs=[h_spec, wa_spec, wb_spec, ba_spec, bb_spec],
        out_specs=out_specs,
        out_shape=out_shape,
        scratch_shapes=[pltpu.VMEM((d, tn), BF16), pltpu.VMEM((d, tn), BF16)],
        compiler_params=_params("arbitrary", "arbitrary"),
        name="proj_glu",
    )(h, w_in, w_in, gb, gb)


def _proj_gate(rows, h, w_in, layer, col0, width):
    d = h.shape[1]
    tn = min(TN_PROJ, width)
    h_spec, w_spec, o_spec = _proj_specs(rows, d, tn, col0, layer)
    return pl.pallas_call(
        _gate_kernel,
        grid=(width // tn, rows.nt),
        in_specs=[h_spec, w_spec],
        out_specs=o_spec,
        out_shape=jax.ShapeDtypeStruct((rows.m, width), BF16),
        scratch_shapes=[pltpu.VMEM((d, tn), BF16)],
        compiler_params=_params("arbitrary", "arbitrary"),
        name="proj_gate",
    )(h, w_in)


def _lambda(lq_ref, lam_init):
    lq = lq_ref[...]
    a = jnp.sum(lq[0:1] * lq[1:2], axis=1, keepdims=True)
    b = jnp.sum(lq[2:3] * lq[3:4], axis=1, keepdims=True)
    return jnp.exp(a) - jnp.exp(b) + lam_init


def _sub_norm(o, g, lam_init):
    ms = jnp.mean(o * o, axis=-1, keepdims=True)
    return o * lax.rsqrt(ms + LN_EPS) * g * (1.0 - lam_init)


def _attn_kernel(lq_ref, g_ref, q_ref, k_ref, vt_ref, o_ref, acc_ref, m_ref, l_ref, *, tq, tk, lam_init):
    i = pl.program_id(2)
    q = q_ref[...]
    lane = lax.broadcasted_iota(jnp.int32, q.shape, 1)
    zero = jnp.zeros_like(q)
    q2 = jnp.concatenate([jnp.where(lane < D_HEAD, q, zero), jnp.where(lane >= D_HEAD, q, zero)], axis=0)
    m_ref[...] = jnp.full(m_ref.shape, NEG_BIG, F32)
    l_ref[...] = jnp.zeros(l_ref.shape, F32)
    acc_ref[...] = jnp.zeros(acc_ref.shape, F32)

    def step(j, masked):
        k0 = pl.multiple_of(j * tk, tk)
        kt = k_ref[pl.ds(k0, tk), :]
        s = lax.dot_general(kt, q2, (((1,), (1,)), ((), ())), preferred_element_type=F32)
        if masked:
            kpos = k0 + lax.broadcasted_iota(jnp.int32, s.shape, 0)
            qpos = i * tq + (lax.broadcasted_iota(jnp.int32, s.shape, 1) & (tq - 1))
            s = jnp.where(kpos <= qpos, s, NEG_BIG)
        m_prev = m_ref[...]
        m_new = jnp.maximum(m_prev, jnp.max(s, axis=0, keepdims=True))
        alpha = jnp.exp2(m_prev - m_new)
        p = jnp.exp2(s - m_new)
        l_ref[...] = alpha * l_ref[...] + jnp.sum(p, axis=0, keepdims=True)
        acc_ref[...] = acc_ref[...] * alpha + _dot(vt_ref[j], p.astype(BF16))
        m_ref[...] = m_new

    n_full = (i * tq) // tk

    def full_step(j, carry):
        step(j, False)
        return carry

    lax.fori_loop(0, n_full, full_step, 0)
    step(n_full, True)

    o2 = acc_ref[...] * (1.0 / l_ref[...])
    lam = _lambda(lq_ref, lam_init)
    ot = o2[:, :tq] - lam * o2[:, tq:]
    o_ref[...] = _sub_norm(ot.T, g_ref[...], lam_init).astype(BF16)


def _prompt_attention(q, k, v, lq, g, batch, seq, lam_init):
    width = q.shape[1]
    nh = width // V_HEAD
    tq = min(TQ_ATTN, seq)
    tk = min(TK_ATTN, seq)
    assert tq & (tq - 1) == 0 and tk % tq == 0 and seq % tk == 0
    q3 = q.reshape(batch, seq, width)
    k3 = k.reshape(batch, seq, width)
    vt = v.reshape(batch, seq // tk, tk, nh, V_HEAD).transpose(0, 3, 1, 4, 2)
    out = pl.pallas_call(
        functools.partial(_attn_kernel, tq=tq, tk=tk, lam_init=lam_init),
        grid=(batch, nh, seq // tq),
        in_specs=[pl.BlockSpec((4, D_HEAD), lambda b, h, i: (0, 0)),
                  pl.BlockSpec((1, V_HEAD), lambda b, h, i: (0, 0)),
                  pl.BlockSpec((None, tq, V_HEAD), lambda b, h, i: (b, i, h)),
                  pl.BlockSpec((None, seq, V_HEAD), lambda b, h, i: (b, 0, h)),
                  pl.BlockSpec((None, None, seq // tk, V_HEAD, tk), lambda b, h, i: (b, h, 0, 0, 0))],
        out_specs=pl.BlockSpec((None, tq, V_HEAD), lambda b, h, i: (b, i, h)),
        out_shape=jax.ShapeDtypeStruct((batch, seq, width), BF16),
        scratch_shapes=[pltpu.VMEM((V_HEAD, 2 * tq), F32),
                        pltpu.VMEM((1, 2 * tq), F32),
                        pltpu.VMEM((1, 2 * tq), F32)],
        compiler_params=_params("arbitrary", "arbitrary", "arbitrary"),
        name="prompt_attention",
    )(lq, g, q3, k3, vt)
    return out.reshape(batch * seq, width)


def _paged_kernel(pt_ref, lq_ref, g_ref, q_ref, kn_ref, vn_ref, *rest, ppg, nj, lam_init):
    del pt_ref
    k_refs, v_refs = rest[:ppg], rest[ppg:2 * ppg]
    o_ref, qmat_ref, s_ref, a_ref, acc_ref = rest[2 * ppg:]
    ph, j = pl.program_id(1), pl.program_id(2)
    page = k_refs[0].shape[0]
    width = k_refs[0].shape[1]
    n_past = nj * ppg
    nh = width // V_HEAD
    row = lax.broadcasted_iota(jnp.int32, (page, width), 0)

    @pl.when((ph == 0) & (j == 0))
    def _():
        comp = lax.broadcasted_iota(jnp.int32, (LANES, width), 1) // D_HEAD
        sub = lax.broadcasted_iota(jnp.int32, (LANES, width), 0)
        qrows = jnp.where(sub == comp, q_ref[...].astype(F32), 0.0)
        qmat_ref[...] = qrows.T.astype(BF16)

    def scores(kp, pg, self_page):
        s = _dot(kp.astype(BF16), qmat_ref[...])
        if self_page:
            s = jnp.where(row[:, :LANES] == 0, s, NEG_BIG)
        s_ref[pl.ds(pl.multiple_of(pg * page, page), page), :] = s

    def weighted(vp, pg):
        a = a_ref[pl.ds(pl.multiple_of(pg * page, page), page), :]
        at = a.T[0:2 * nh, :].astype(BF16)
        acc_ref[...] += _dot(at, vp.astype(BF16))

    @pl.when(ph == 0)
    def _():
        for gi in range(ppg):
            scores(k_refs[gi][...], j * ppg + gi, False)

    @pl.when((ph == 0) & (j == nj - 1))
    def _():
        scores(jnp.where(row == 0, kn_ref[...], 0.0), n_past, True)
        s = s_ref[...]
        m = jnp.max(s, axis=0, keepdims=True)
        p = jnp.exp2(s - m)
        a = p * (1.0 / jnp.sum(p, axis=0, keepdims=True))
        lam = _lambda(lq_ref, lam_init)
        a_ref[...] = a - lam * pltpu.roll(a, LANES - 1, 1)
        acc_ref[...] = jnp.zeros(acc_ref.shape, F32)

    @pl.when(ph == 1)
    def _():
        for gi in range(ppg):
            weighted(v_refs[gi][...], j * ppg + gi)

    @pl.when((ph == 1) & (j == nj - 1))
    def _():
        weighted(jnp.where(row == 0, vn_ref[...], 0.0), n_past)
        for h in range(nh):
            ls = slice(h * V_HEAD, (h + 1) * V_HEAD)
            o = acc_ref[2 * h:2 * h + 1, ls]
            o_ref[:, ls] = _sub_norm(o, g_ref[...], lam_init).astype(BF16)


def _paged_attention(q, k_new, v_new, cache_k, cache_v, pages, lq, g, lam_init):
    bs, width = q.shape
    n_pages = pages.shape[1]
    page = cache_k.shape[1]
    ppg = math.gcd(PAGES_PER_STEP, n_pages)
    nj = n_pages // ppg
    nh = width // V_HEAD

    def k_map(gi):
        return lambda b, ph, j, pt: (pt[b * n_pages + jnp.where(ph == 0, j, nj - 1) * ppg + gi], 0, 0)

    def v_map(gi):
        return lambda b, ph, j, pt: (pt[b * n_pages + jnp.where(ph == 0, 0, j) * ppg + gi], 0, 0)

    row_spec = pl.BlockSpec((None, 1, width), lambda b, ph, j, pt: (b, 0, 0))
    in_specs = [pl.BlockSpec((4, D_HEAD), lambda b, ph, j, pt: (0, 0)),
                pl.BlockSpec((1, V_HEAD), lambda b, ph, j, pt: (0, 0)),
                row_spec, row_spec, row_spec]
    in_specs += [pl.BlockSpec((None, page, width), k_map(gi)) for gi in range(ppg)]
    in_specs += [pl.BlockSpec((None, page, width), v_map(gi)) for gi in range(ppg)]
    n_rows = (n_pages + 1) * page
    out = pl.pallas_call(
        functools.partial(_paged_kernel, ppg=ppg, nj=nj, lam_init=lam_init),
        grid_spec=pltpu.PrefetchScalarGridSpec(
            num_scalar_prefetch=1,
            grid=(bs, 2, nj),
            in_specs=in_specs,
            out_specs=row_spec,
            scratch_shapes=[pltpu.VMEM((width, LANES), BF16),
                            pltpu.VMEM((n_rows, LANES), F32),
                            pltpu.VMEM((n_rows, LANES), F32),
                            pltpu.VMEM((2 * nh, width), F32)]),
        out_shape=jax.ShapeDtypeStruct((bs, 1, width), BF16),
        compiler_params=_params("arbitrary", "arbitrary", "arbitrary"),
        name="paged_attention",
    )(pages.reshape(-1), lq, g, q.reshape(bs, 1, width), k_new.reshape(bs, 1, width),
      v_new.reshape(bs, 1, width), *([cache_k] * ppg), *([cache_v] * ppg))
    return out.reshape(bs, width)


HALO_ROWS = 32


def _conv_seq_kernel(u_ref, halo_ref, w_ref, b_ref, g_ref, beta_ref, cv_ref, ext_ref, y_ref, *, tpb):
    tm, d = u_ref.shape
    first = (pl.program_id(0) % tpb) == 0
    ext_ref[0:HALO_ROWS, :] = jnp.where(first, 0.0, halo_ref[...])
    ext_ref[HALO_ROWS:, :] = u_ref[...]
    lead = HALO_ROWS - (CONV_K - 1)
    rc = min(CONV_ROWS, tm)

    def chunk(r, carry):
        r0 = pl.multiple_of(r * rc, rc)
        for j in range(d // LANES):
            ls = slice(j * LANES, (j + 1) * LANES)
            blk = ext_ref[pl.ds(r0, rc + HALO_ROWS), ls]
            acc = jnp.zeros((rc, LANES), F32)
            for s in range(8):
                taps = [k for k in range(CONV_K) if (lead + k) % 8 == s]
                if not taps:
                    continue
                span = (lead + taps[-1]) - s + rc
                shifted = blk[s:s + span]
                for k in taps:
                    a = lead + k - s
                    acc = acc + w_ref[k:k + 1, ls] * shifted[a:a + rc]
            y_ref[pl.ds(r0, rc), ls] = acc + b_ref[:, ls]
        return carry

    lax.fori_loop(0, tm // rc, chunk, 0)
    cv_ref[...] = _silu(_layer_norm(y_ref[...], g_ref[...], beta_ref[...])).astype(BF16)


def _conv_seq(rows, u, w, b, g, beta):
    d = u.shape[1]
    tm = rows.tm
    const = lambda i: (0, 0)
    return pl.pallas_call(
        functools.partial(_conv_seq_kernel, tpb=rows.tpb),
        grid=(rows.nt,),
        in_specs=[pl.BlockSpec((tm, d), lambda i: (i, 0)),
                  pl.BlockSpec((HALO_ROWS, d), lambda i: (jnp.maximum(i * (tm // HALO_ROWS) - 1, 0), 0)),
                  pl.BlockSpec((CONV_K, d), const),
                  pl.BlockSpec((1, d), const), pl.BlockSpec((1, d), const), pl.BlockSpec((1, d), const)],
        out_specs=pl.BlockSpec((tm, d), lambda i: (i, 0)),
        out_shape=jax.ShapeDtypeStruct(u.shape, BF16),
        scratch_shapes=[pltpu.VMEM((tm + HALO_ROWS, d), F32), pltpu.VMEM((tm, d), F32)],
        compiler_params=_params("arbitrary"),
        name="conv_seq",
    )(u, u, w, b, g, beta)


def _conv_step_kernel(st_ref, u_ref, w_ref, b_ref, g_ref, beta_ref, cv_ref, acc_ref):
    k = pl.program_id(1)
    wk = w_ref[pl.ds(k, 1), :]

    @pl.when(k == 0)
    def _():
        acc_ref[...] = jnp.zeros(acc_ref.shape, F32)

    @pl.when(k < CONV_K - 1)
    def _():
        acc_ref[...] += st_ref[...] * wk

    @pl.when(k == CONV_K - 1)
    def _():
        y = acc_ref[...] + u_ref[...] * wk + b_ref[...]
        cv_ref[...] = _silu(_layer_norm(y, g_ref[...], beta_ref[...])).astype(BF16)


def _conv_step(rows, state, u, w, b, g, beta):
    d = u.shape[1]
    tm = rows.tm
    const = lambda i, k: (0, 0)
    row_spec = pl.BlockSpec((tm, d), lambda i, k: (i, 0))
    return pl.pallas_call(
        _conv_step_kernel,
        grid=(rows.nt, CONV_K),
        in_specs=[pl.BlockSpec((tm, d), lambda i, k: (i, jnp.minimum(k, CONV_K - 2))),
                  row_spec,
                  pl.BlockSpec((CONV_K, d), const),
                  pl.BlockSpec((1, d), const), pl.BlockSpec((1, d), const), pl.BlockSpec((1, d), const)],
        out_specs=row_spec,
        out_shape=jax.ShapeDtypeStruct(u.shape, BF16),
        scratch_shapes=[pltpu.VMEM((tm, d), F32)],
        compiler_params=_params("arbitrary", "arbitrary"),
        name="conv_step",
    )(state.reshape(state.shape[0], (CONV_K - 1) * d), u, w, b, g, beta)


def _mix_kernel(cv_ref, ga_ref, gb_ref, at_ref, x_ref, g1_ref, sc2_ref, sh2_ref, wp_ref, bp_ref, wo_ref,
                lg_ref, lb_ref, x1_ref, h2_ref, wpbf_ref, wobf_ref, *, alpha):
    @pl.when(pl.program_id(0) == 0)
    def _():
        wpbf_ref[...] = wp_ref[...].astype(BF16)
        wobf_ref[...] = wo_ref[...].astype(BF16)

    conv_out = _dot(cv_ref[...], wpbf_ref[...]) + bp_ref[...]
    merged = ga_ref[...].astype(F32) * conv_out + gb_ref[...].astype(F32) * at_ref[...].astype(F32)
    mix = _dot(merged.astype(BF16), wobf_ref[...])
    x1 = _layer_norm(alpha * x_ref[...] + g1_ref[...] * mix, lg_ref[...], lb_ref[...])
    x1_ref[...] = x1
    h2_ref[...] = (x1 * (1.0 + sc2_ref[...]) + sh2_ref[...]).astype(BF16)


def _mix(rows, cv, gates, attn, x, g1, sc2, sh2, w_pw2, b_pw2, w_o, ln_g, ln_b, alpha):
    d = x.shape[1]
    tm = rows.tm
    row_spec = pl.BlockSpec((tm, d), lambda i: (i, 0))
    const = lambda i: (0, 0)
    vec = pl.BlockSpec((1, d), const)
    mat = pl.BlockSpec((d, d), const)
    mod = rows.mod_spec(d, 1)
    return pl.pallas_call(
        functools.partial(_mix_kernel, alpha=alpha),
        grid=(rows.nt,),
        in_specs=[row_spec, row_spec, pl.BlockSpec((tm, d), lambda i: (i, 1)), row_spec, row_spec,
                  mod, mod, mod, mat, vec, mat, vec, vec],
        out_specs=[row_spec, row_spec],
        out_shape=[jax.ShapeDtypeStruct(x.shape, F32), jax.ShapeDtypeStruct(x.shape, BF16)],
        scratch_shapes=[pltpu.VMEM((d, d), BF16), pltpu.VMEM((d, d), BF16)],
        compiler_params=_params("arbitrary"),
        name="mix",
    )(cv, gates, gates, attn, x, g1, sc2, sh2, w_pw2, b_pw2, w_o, ln_g, ln_b)


FFN_HALO_ROWS = 8


def _ffn_in_kernel(h_ref, wg_ref, wu_ref, gt_ref, up_ref, *rest, tail):
    h = h_ref[...]
    gt = _dot(h, wg_ref[...])
    gt_ref[...] = gt.astype(gt_ref.dtype)
    up_ref[...] = _dot(h, wu_ref[...]).astype(BF16)
    if tail:
        rest[0][...] = gt[gt.shape[0] - tail:, :]


def _ffn_in(rows, h, w_ffn_in_bf, layer, ffn, gt_dtype, tail):
    d = h.shape[1]
    tm = rows.tm
    tn = ffn // 2 if (ffn // 2) % LANES == 0 else ffn
    nj = ffn // tn
    o_spec = pl.BlockSpec((tm, tn), lambda j, i: (i, j))
    out_specs = [o_spec, o_spec]
    out_shape = [jax.ShapeDtypeStruct((rows.m, ffn), gt_dtype), jax.ShapeDtypeStruct((rows.m, ffn), BF16)]
    if tail:
        out_specs.append(pl.BlockSpec((None, tail, tn), lambda j, i: (i // rows.tpb, 0, j)))
        out_shape.append(jax.ShapeDtypeStruct((rows.m // rows.seq, tail, ffn), F32))
    return pl.pallas_call(
        functools.partial(_ffn_in_kernel, tail=tail),
        grid=(nj, rows.nt),
        in_specs=[pl.BlockSpec((tm, d), lambda j, i: (i, 0)),
                  pl.BlockSpec((None, d, tn), lambda j, i: (layer, 0, j)),
                  pl.BlockSpec((None, d, tn), lambda j, i: (layer, 0, nj + j))],
        out_specs=out_specs,
        out_shape=out_shape,
        compiler_params=_params("arbitrary", "arbitrary"),
        name="ffn_in",
    )(h, w_ffn_in_bf, w_ffn_in_bf)


def _ffn_tail(act, x_ref, g2_ref, wd_ref, lg_ref, lb_ref, x2_ref, nxt, alpha):
    f = _dot(act.astype(BF16), wd_ref[...])
    x2 = _layer_norm(alpha * x_ref[...] + g2_ref[...] * f, lg_ref[...], lb_ref[...])
    x2_ref[...] = x2
    if nxt:
        sc_ref, sh_ref, hn_ref = nxt
        hn_ref[...] = (x2 * (1.0 + sc_ref[...]) + sh_ref[...]).astype(BF16)


def _ffn_down_seq_kernel(gt_ref, halo_ref, up_ref, x_ref, g2_ref, cw_ref, cb_ref, wd_ref, lg_ref, lb_ref,
                         *rest, tpb, alpha, has_next):
    if has_next:
        sc_ref, sh_ref, x2_ref, hn_ref, ext_ref = rest
        nxt = (sc_ref, sh_ref, hn_ref)
    else:
        x2_ref, ext_ref = rest
        nxt = None
    tm = gt_ref.shape[0]
    first = (pl.program_id(0) % tpb) == 0
    ext_ref[0:FFN_HALO_ROWS, :] = jnp.where(first, 0.0, halo_ref[...].astype(F32))
    ext_ref[FFN_HALO_ROWS:, :] = gt_ref[...].astype(F32)
    lead = FFN_HALO_ROWS - (FFN_CONV_K - 1)
    conv = cb_ref[...]
    for k in range(FFN_CONV_K):
        conv = conv + cw_ref[k:k + 1, :] * ext_ref[lead + k:lead + k + tm, :]
    _ffn_tail(_silu(conv) * up_ref[...].astype(F32), x_ref, g2_ref, wd_ref, lg_ref, lb_ref, x2_ref, nxt, alpha)


def _ffn_down_step_kernel(gt_ref, p2_ref, p1_ref, up_ref, x_ref, g2_ref, cw_ref, cb_ref, wd_ref, lg_ref, lb_ref,
                          *rest, alpha, has_next):
    if has_next:
        sc_ref, sh_ref, x2_ref, hn_ref = rest
        nxt = (sc_ref, sh_ref, hn_ref)
    else:
        (x2_ref,) = rest
        nxt = None
    conv = (cb_ref[...] + cw_ref[0:1, :] * p2_ref[...] + cw_ref[1:2, :] * p1_ref[...]
            + cw_ref[2:3, :] * gt_ref[...].astype(F32))
    _ffn_tail(_silu(conv) * up_ref[...].astype(F32), x_ref, g2_ref, wd_ref, lg_ref, lb_ref, x2_ref, nxt, alpha)


def _ffn_down(rows, gt, up, x, g2, conv_w, conv_b, w_down_bf, ln_g, ln_b, alpha, nxt, prev):
    m, d = x.shape
    ffn = gt.shape[1]
    tm = min(TM_FFN_DOWN, rows.tm)
    sub = rows.tm // tm
    tpb = rows.tpb * sub if rows.mod_rows == 1 else 1
    row_d = pl.BlockSpec((tm, d), lambda i: (i, 0))
    row_f = pl.BlockSpec((tm, ffn), lambda i: (i, 0))
    const = lambda i: (0, 0)
    vec_d = pl.BlockSpec((1, d), const)
    if rows.mod_rows == 1:
        mod = pl.BlockSpec((None, 1, d), lambda i: (i // tpb, 0, 0))
        mods = lambda a: a
    else:
        mod = pl.BlockSpec((tm, d), lambda i: (i, 0))
        mods = lambda a: a.reshape(m, d)
    common_in = [row_f, row_d, mod, pl.BlockSpec((FFN_CONV_K, ffn), const), pl.BlockSpec((1, ffn), const),
                 pl.BlockSpec((ffn, d), const), vec_d, vec_d]
    common_args = [up, x, mods(g2), conv_w, conv_b, w_down_bf, ln_g, ln_b]
    out_specs = [row_d]
    out_shape = [jax.ShapeDtypeStruct((m, d), F32)]
    if nxt:
        common_in += [mod, mod]
        common_args += [mods(nxt[0]), mods(nxt[1])]
        out_specs.append(row_d)
        out_shape.append(jax.ShapeDtypeStruct((m, d), BF16))
    if prev is None:
        body = functools.partial(_ffn_down_seq_kernel, tpb=tpb, alpha=alpha, has_next=bool(nxt))
        lead_in = [row_f, pl.BlockSpec((FFN_HALO_ROWS, ffn),
                                       lambda i: (jnp.maximum(i * (tm // FFN_HALO_ROWS) - 1, 0), 0))]
        lead_args = [gt, gt]
        scratch = [pltpu.VMEM((tm + FFN_HALO_ROWS, ffn), F32)]
    else:
        body = functools.partial(_ffn_down_step_kernel, alpha=alpha, has_next=bool(nxt))
        lead_in = [row_f, row_f, row_f]
        lead_args = [gt, prev[0], prev[1]]
        scratch = []
    res = pl.pallas_call(
        body,
        grid=(m // tm,),
        in_specs=lead_in + common_in,
        out_specs=out_specs,
        out_shape=out_shape,
        scratch_shapes=scratch,
        compiler_params=_params("arbitrary"),
        name="ffn_down",
    )(*lead_args, *common_args)
    return (res[0], res[1]) if nxt else (res[0], None)


def _rope_tables(pos):
    half = ROT_DIM // 2
    inv = ROPE_THETA ** (-jnp.arange(half, dtype=F32) / half)
    ang = pos.astype(F32)[:, None] * inv[None, :]
    cos, sin = jnp.cos(ang), jnp.sin(ang)
    t = pos.shape[0]
    zeros = jnp.zeros((t, half), F32)
    rest = D_HEAD - ROT_DIM
    c = jnp.concatenate([cos, cos, jnp.ones((t, rest), F32)], axis=1)
    sa = jnp.concatenate([-sin, zeros, jnp.zeros((t, rest), F32)], axis=1)
    sb = jnp.concatenate([zeros, sin, jnp.zeros((t, rest), F32)], axis=1)
    reps = LANES // D_HEAD
    return tuple(jnp.tile(a, (1, reps)) for a in (c, sa, sb))


def kernel(x_prompt, x_sample, cache_k, cache_v, state_conv, state_ffn_conv, page_table, c_prompt, c_sample,
           w_ada, b_ada, w_in, lambda_qk, attn_subln_g, glu_b, dw_w, dw_b, conv_ln_g, conv_ln_b, w_pw2, b_pw2,
           w_o, ln1_g, ln1_b, w_ffn_in, ffn_dw_w, ffn_dw_b, w_down, ln2_g, ln2_b):
    depth, d, _ = w_in.shape
    bp, seq, _ = x_prompt.shape
    bs, dec_seq, _ = x_sample.shape
    assert dec_seq == 1
    n_heads = cache_v.shape[3]
    qk_width = cache_k.shape[3] * cache_k.shape[4]
    v_width = n_heads * cache_v.shape[4]
    conv_ch = state_conv.shape[-1]
    ffn = state_ffn_conv.shape[-1]
    assert cache_k.shape[4] == D_HEAD and cache_v.shape[4] == V_HEAD and qk_width == v_width
    assert state_conv.shape[2] == CONV_K - 1 and state_ffn_conv.shape[2] == FFN_CONV_K - 1
    n_phys, page = cache_k.shape[1], cache_k.shape[2]
    past = page_table.shape[1] * page
    alpha = (2 * depth) ** 0.25
    q_scale = (D_HEAD ** -0.5) * LOG2E

    rows_p = _Rows(bp * seq, min(TM_PROMPT, seq), seq)
    rows_s = _Rows(bs, bs, 1)

    n_c = bp + bs
    pad = (-n_c) % 16
    c_all = jnp.concatenate([c_prompt, c_sample, jnp.zeros((pad, d), F32)], axis=0)
    ada = _ada(c_all, w_ada, b_ada)

    def mods(layer, rows, lo, hi):
        return [rows.mod(a) for a in jnp.split(ada[layer, lo:hi], 6, axis=-1)]

    mods_p = [mods(l, rows_p, 0, bp) for l in range(depth)]
    mods_s = [mods(l, rows_s, bp, n_c) for l in range(depth)]

    tables_p = _rope_tables(jnp.arange(seq))
    tables_s = _rope_tables(jnp.tile(past + jnp.arange(dec_seq), bs))

    cache_k2 = cache_k.reshape(depth * n_phys, page, qk_width)
    cache_v2 = cache_v.reshape(depth * n_phys, page, v_width)
    w_ffn_in_bf = w_ffn_in.astype(BF16)
    w_down_bf = w_down.astype(BF16)
    row2 = lambda a: a.reshape(a.shape[0], 1, a.shape[1])

    def layer(l, rows, x, h, mod, nxt, tables, sample):
        sh1, sc1, g1, sh2, sc2, g2 = mod
        del sh1, sc1
        q = _proj_q(rows, h, w_in, l, tables, qk_width, q_scale)
        k, k_bf = _proj_kv(rows, h, w_in, l, qk_width, qk_width, tables)
        v, v_bf = _proj_kv(rows, h, w_in, l, 2 * qk_width, v_width, None)
        col = 2 * qk_width + v_width
        lam_init = 0.8 - 0.6 * math.exp(-0.3 * l)
        lq, sub_g = lambda_qk[l], attn_subln_g[l].reshape(1, V_HEAD)
        conv_args = (dw_w[l], row2(dw_b)[l], row2(conv_ln_g)[l], row2(conv_ln_b)[l])
        if sample:
            (u,) = _proj_glu(rows, h, w_in, glu_b, l, col, conv_ch, 0)
            attn = _paged_attention(q, k, v, cache_k2, cache_v2, page_table + l * n_phys, lq, sub_g, lam_init)
            cv = _conv_step(rows, state_conv[l], u, *conv_args)
            conv_hist = jnp.concatenate([state_conv[l][:, 1:], u[:, None, :]], axis=1)
        else:
            u, u_tail = _proj_glu(rows, h, w_in, glu_b, l, col, conv_ch, HALO_ROWS)
            attn = _prompt_attention(q, k_bf, v_bf, lq, sub_g, bp, seq, lam_init)
            cv = _conv_seq(rows, u, *conv_args)
            conv_hist = u_tail[:, HALO_ROWS - (CONV_K - 1):]
        gates = _proj_gate(rows, h, w_in, l, col + 2 * conv_ch, 2 * d)
        x1, h2 = _mix(rows, cv, gates, attn, x, g1, sc2, sh2, w_pw2[l], row2(b_pw2)[l], w_o[l],
                      row2(ln1_g)[l], row2(ln1_b)[l], alpha)
        ffn_args = (ffn_dw_w[l], row2(ffn_dw_b)[l], w_down_bf[l], row2(ln2_g)[l], row2(ln2_b)[l], alpha, nxt)
        if sample:
            gt, up = _ffn_in(rows, h2, w_ffn_in_bf, l, ffn, F32, 0)
            hist = state_ffn_conv[l]
            x2, h_next = _ffn_down(rows, gt, up, x1, g2, *ffn_args, (hist[:, 0], hist[:, 1]))
            ffn_hist = jnp.concatenate([hist[:, 1:], gt[:, None, :]], axis=1)
        else:
            gt, up, gt_tail = _ffn_in(rows, h2, w_ffn_in_bf, l, ffn, BF16, FFN_HALO_ROWS)
            x2, h_next = _ffn_down(rows, gt, up, x1, g2, *ffn_args, None)
            ffn_hist = gt_tail[:, FFN_HALO_ROWS - (FFN_CONV_K - 1):]
        return x2, h_next, k, v, conv_hist, ffn_hist

    xp = x_prompt.reshape(bp * seq, d)
    xs = x_sample.reshape(bs, d)
    hp = _modulate(rows_p, xp, mods_p[0][1], mods_p[0][0])
    hs = _modulate(rows_s, xs, mods_s[0][1], mods_s[0][0])
    outs_p, outs_s = [], []
    for l in range(depth):
        nxt_p = (mods_p[l + 1][1], mods_p[l + 1][0]) if l + 1 < depth else None
        nxt_s = (mods_s[l + 1][1], mods_s[l + 1][0]) if l + 1 < depth else None
        xp, hp, *rest = layer(l, rows_p, xp, hp, mods_p[l], nxt_p, tables_p, False)
        outs_p.append(rest)
        xs, hs, *rest = layer(l, rows_s, xs, hs, mods_s[l], nxt_s, tables_s, True)
        outs_s.append(rest)

    def stack(outs, i, shape):
        return jnp.stack([o[i].reshape(shape) for o in outs])

    kd = cache_k.shape[3]
    return (xp.reshape(bp, seq, d), xs.reshape(bs, dec_seq, d),
            stack(outs_p, 0, (bp, seq, kd, D_HEAD)), stack(outs_p, 1, (bp, seq, n_heads, V_HEAD)),
            stack(outs_p, 2, (bp, CONV_K - 1, conv_ch)), stack(outs_p, 3, (bp, FFN_CONV_K - 1, ffn)),
            stack(outs_s, 0, (bs, dec_seq, kd, D_HEAD)), stack(outs_s, 1, (bs, dec_seq, n_heads, V_HEAD)),
            stack(outs_s, 2, (bs, CONV_K - 1, conv_ch)), stack(outs_s, 3, (bs, FFN_CONV_K - 1, ffn)))
```

```python
import functools
import math

import jax
import jax.numpy as jnp
from jax import lax
from jax.experimental import pallas as pl
from jax.experimental.pallas import tpu as pltpu

BF16 = jnp.bfloat16
F32 = jnp.float32

D_HEAD = 64
V_HEAD = 2 * D_HEAD
ROT_DIM = D_HEAD // 4
ROPE_THETA = 500000.0
CONV_K = 31
FFN_CONV_K = 3
LN_EPS = 1e-5
LANES = 128
NEG_BIG = -1e30
LOG2E = 1.4426950408889634
VMEM_LIMIT_BYTES = 56 * 1024 * 1024

TM_PROMPT = 1024
TN_PROJ = 1024
TM_FFN_DOWN = 512
TQ_ATTN = 256
TK_ATTN = 512
CONV_ROWS = 64
PAGES_PER_STEP = 8
HEADS_PER_STEP = 2


def _params(*sem):
    return pltpu.CompilerParams(dimension_semantics=sem, vmem_limit_bytes=VMEM_LIMIT_BYTES)


def _silu(z):
    return z * jax.nn.sigmoid(z)


def _layer_norm(y, g, b):
    mu = jnp.mean(y, axis=-1, keepdims=True)
    d = y - mu
    var = jnp.mean(d * d, axis=-1, keepdims=True)
    return d * lax.rsqrt(var + LN_EPS) * g + b


def _dot(a, b):
    return jnp.dot(a, b, preferred_element_type=F32)


def _ada_kernel(c_ref, w_ref, b_ref, o_ref):
    s = _silu(c_ref[...]).astype(BF16)
    o_ref[...] = _dot(s, w_ref[...].astype(BF16)) + b_ref[...]


def _ada(c_all, w_ada, b_ada):
    nl, d, n = w_ada.shape
    mc = c_all.shape[0]
    tn = 1536 if n % 1536 == 0 else n
    return pl.pallas_call(
        _ada_kernel,
        grid=(nl, n // tn),
        in_specs=[pl.BlockSpec((mc, d), lambda l, j: (0, 0)),
                  pl.BlockSpec((None, d, tn), lambda l, j: (l, 0, j)),
                  pl.BlockSpec((None, 1, tn), lambda l, j: (l, 0, j))],
        out_specs=pl.BlockSpec((None, mc, tn), lambda l, j: (l, 0, j)),
        out_shape=jax.ShapeDtypeStruct((nl, mc, n), F32),
        compiler_params=_params("arbitrary", "arbitrary"),
        name="ada",
    )(c_all, w_ada, b_ada.reshape(nl, 1, n))


class _Rows:
    def __init__(self, m, tm, seq):
        self.m, self.tm, self.seq = m, tm, seq
        self.nt = m // tm
        if seq >= tm:
            assert seq % tm == 0
            self.tpb, self.mod_rows = seq // tm, 1
        else:
            assert seq == 1
            self.tpb, self.mod_rows = 1, tm

    def mod(self, a):
        return a.reshape(-1, self.mod_rows, a.shape[-1])

    def mod_spec(self, d, grid_rank):
        if grid_rank == 1:
            return pl.BlockSpec((None, self.mod_rows, d), lambda i: (i // self.tpb, 0, 0))
        return pl.BlockSpec((None, self.mod_rows, d), lambda j, i: (i // self.tpb, 0, 0))


def _modulate_kernel(x_ref, sc_ref, sh_ref, h_ref):
    h_ref[...] = (x_ref[...] * (1.0 + sc_ref[...]) + sh_ref[...]).astype(BF16)


def _modulate(rows, x, sc, sh):
    d = x.shape[1]
    row_spec = pl.BlockSpec((rows.tm, d), lambda i: (i, 0))
    return pl.pallas_call(
        _modulate_kernel,
        grid=(rows.nt,),
        in_specs=[row_spec, rows.mod_spec(d, 1), rows.mod_spec(d, 1)],
        out_specs=row_spec,
        out_shape=jax.ShapeDtypeStruct(x.shape, BF16),
        compiler_params=_params("arbitrary"),
        name="modulate",
    )(x, sc, sh)


def _load_weight(w_ref, wbf_ref):
    @pl.when(pl.program_id(1) == 0)
    def _():
        wbf_ref[...] = w_ref[...].astype(BF16)


def _rope_chunk(y, c, sa, sb):
    return y * c + pltpu.roll(y, LANES - ROT_DIM // 2, 1) * sa + pltpu.roll(y, ROT_DIM // 2, 1) * sb


def _q_kernel(h_ref, w_ref, c_ref, sa_ref, sb_ref, q_ref, wbf_ref, *, scale):
    _load_weight(w_ref, wbf_ref)
    y = _dot(h_ref[...], wbf_ref[...])
    c, sa, sb = c_ref[...], sa_ref[...], sb_ref[...]
    for j in range(y.shape[1] // LANES):
        ls = slice(j * LANES, (j + 1) * LANES)
        q_ref[:, ls] = (_rope_chunk(y[:, ls], c, sa, sb) * scale).astype(BF16)


def _kv_kernel(h_ref, w_ref, *rest, rope):
    if rope:
        c_ref, sa_ref, sb_ref, o_ref, obf_ref, wbf_ref = rest
    else:
        o_ref, obf_ref, wbf_ref = rest
    _load_weight(w_ref, wbf_ref)
    y = _dot(h_ref[...], wbf_ref[...])
    if rope:
        c, sa, sb = c_ref[...], sa_ref[...], sb_ref[...]
        for j in range(y.shape[1] // LANES):
            ls = slice(j * LANES, (j + 1) * LANES)
            r = _rope_chunk(y[:, ls], c, sa, sb)
            o_ref[:, ls] = r
            obf_ref[:, ls] = r.astype(BF16)
    else:
        o_ref[...] = y
        obf_ref[...] = y.astype(BF16)


def _glu_kernel(h_ref, wa_ref, wb_ref, ba_ref, bb_ref, u_ref, *rest, tail):
    if tail:
        tail_ref, wabf_ref, wbbf_ref = rest
    else:
        wabf_ref, wbbf_ref = rest
    _load_weight(wa_ref, wabf_ref)
    _load_weight(wb_ref, wbbf_ref)
    h = h_ref[...]
    a = _dot(h, wabf_ref[...]) + ba_ref[...]
    b = _dot(h, wbbf_ref[...]) + bb_ref[...]
    u = a * jax.nn.sigmoid(b)
    u_ref[...] = u
    if tail:
        tail_ref[...] = u[u.shape[0] - tail:, :]


def _gate_kernel(h_ref, w_ref, o_ref, wbf_ref):
    _load_weight(w_ref, wbf_ref)
    o_ref[...] = jax.nn.sigmoid(_dot(h_ref[...], wbf_ref[...])).astype(BF16)


def _proj_specs(rows, d, tn, col0, layer):
    h_spec = pl.BlockSpec((rows.tm, d), lambda j, i: (i, 0))
    w_spec = pl.BlockSpec((None, d, tn), lambda j, i: (layer, 0, col0 // tn + j))
    o_spec = pl.BlockSpec((rows.tm, tn), lambda j, i: (i, j))
    return h_spec, w_spec, o_spec


def _table_specs(rows, tables):
    nt = tables[0].shape[0] // rows.tm
    return [pl.BlockSpec((rows.tm, LANES), lambda j, i: (i % nt, 0))] * 3


def _proj_q(rows, h, w_in, layer, tables, width, scale):
    d = h.shape[1]
    tn = min(TN_PROJ, width)
    h_spec, w_spec, o_spec = _proj_specs(rows, d, tn, 0, layer)
    return pl.pallas_call(
        functools.partial(_q_kernel, scale=scale),
        grid=(width // tn, rows.nt),
        in_specs=[h_spec, w_spec] + _table_specs(rows, tables),
        out_specs=o_spec,
        out_shape=jax.ShapeDtypeStruct((rows.m, width), BF16),
        scratch_shapes=[pltpu.VMEM((d, tn), BF16)],
        compiler_params=_params("arbitrary", "arbitrary"),
        name="proj_q",
    )(h, w_in, *tables)


def _proj_kv(rows, h, w_in, layer, col0, width, tables):
    d = h.shape[1]
    tn = min(TN_PROJ, width)
    h_spec, w_spec, o_spec = _proj_specs(rows, d, tn, col0, layer)
    rope = tables is not None
    extra = _table_specs(rows, tables) if rope else []
    args = tuple(tables) if rope else ()
    return pl.pallas_call(
        functools.partial(_kv_kernel, rope=rope),
        grid=(width // tn, rows.nt),
        in_specs=[h_spec, w_spec] + extra,
        out_specs=[o_spec, o_spec],
        out_shape=[jax.ShapeDtypeStruct((rows.m, width), F32),
                   jax.ShapeDtypeStruct((rows.m, width), BF16)],
        scratch_shapes=[pltpu.VMEM((d, tn), BF16)],
        compiler_params=_params("arbitrary", "arbitrary"),
        name="proj_k" if rope else "proj_v",
    )(h, w_in, *args)


def _proj_glu(rows, h, w_in, glu_b, layer, col0, width, tail):
    d = h.shape[1]
    tn = min(TN_PROJ, width)
    nl = glu_b.shape[0]
    h_spec, wa_spec, o_spec = _proj_specs(rows, d, tn, col0, layer)
    _, wb_spec, _ = _proj_specs(rows, d, tn, col0 + width, layer)
    ba_spec = pl.BlockSpec((None, 1, tn), lambda j, i: (layer, 0, j))
    bb_spec = pl.BlockSpec((None, 1, tn), lambda j, i: (layer, 0, width // tn + j))
    out_specs = [o_spec]
    out_shape = [jax.ShapeDtypeStruct((rows.m, width), F32)]
    if tail:
        out_specs.append(pl.BlockSpec((None, tail, tn), lambda j, i: (i // rows.tpb, 0, j)))
        out_shape.append(jax.ShapeDtypeStruct((rows.m // rows.seq, tail, width), F32))
    gb = glu_b.reshape(nl, 1, 2 * width)
    return pl.pallas_call(
        functools.partial(_glu_kernel, tail=tail),
        grid=(width // tn, rows.nt),
        in_specs=[h_spec, wa_spec, wb_spec, ba_spec, bb_spec],
        out_specs=out_specs,
        out_shape=out_shape,
        scratch_shapes=[pltpu.VMEM((d, tn), BF16), pltpu.VMEM((d, tn), BF16)],
        compiler_params=_params("arbitrary", "arbitrary"),
        name="proj_glu",
    )(h, w_in, w_in, gb, gb)


def _proj_gate(rows, h, w_in, layer, col0, width):
    d = h.shape[1]
    tn = min(TN_PROJ, width)
    h_spec, w_spec, o_spec = _proj_specs(rows, d, tn, col0, layer)
    return pl.pallas_call(
        _gate_kernel,
        grid=(width // tn, rows.nt),
        in_specs=[h_spec, w_spec],
        out_specs=o_spec,
        out_shape=jax.ShapeDtypeStruct((rows.m, width), BF16),
        scratch_shapes=[pltpu.VMEM((d, tn), BF16)],
        compiler_params=_params("arbitrary", "arbitrary"),
        name="proj_gate",
    )(h, w_in)


def _lambda(lq_ref, lam_init):
    lq = lq_ref[...]
    a = jnp.sum(lq[0:1] * lq[1:2], axis=1, keepdims=True)
    b = jnp.sum(lq[2:3] * lq[3:4], axis=1, keepdims=True)
    return jnp.exp(a) - jnp.exp(b) + lam_init


def _sub_norm(o, g, lam_init):
    ms = jnp.mean(o * o, axis=-1, keepdims=True)
    return o * lax.rsqrt(ms + LN_EPS) * g * (1.0 - lam_init)


def _attn_kernel(lq_ref, g_ref, q_ref, k_ref, vt_ref, o_ref, acc_ref, m_ref, l_ref, *, tq, tk, hps, lam_init):
    i = pl.program_id(2)
    q2 = []
    for hh in range(hps):
        q = q_ref[:, hh * V_HEAD:(hh + 1) * V_HEAD]
        lane = lax.broadcasted_iota(jnp.int32, q.shape, 1)
        zero = jnp.zeros_like(q)
        q2.append(jnp.concatenate([jnp.where(lane < D_HEAD, q, zero), jnp.where(lane >= D_HEAD, q, zero)],
                                  axis=0))
    m_ref[...] = jnp.full(m_ref.shape, NEG_BIG, F32)
    l_ref[...] = jnp.zeros(l_ref.shape, F32)
    acc_ref[...] = jnp.zeros(acc_ref.shape, F32)

    def step(j, masked):
        k0 = pl.multiple_of(j * tk, tk)
        for hh in range(hps):
            kt = k_ref[pl.ds(k0, tk), hh * V_HEAD:(hh + 1) * V_HEAD]
            s = lax.dot_general(kt, q2[hh], (((1,), (1,)), ((), ())), preferred_element_type=F32)
            if masked:
                kpos = k0 + lax.broadcasted_iota(jnp.int32, s.shape, 0)
                qpos = i * tq + (lax.broadcasted_iota(jnp.int32, s.shape, 1) & (tq - 1))
                s = jnp.where(kpos <= qpos, s, NEG_BIG)
            m_prev = m_ref[hh]
            m_new = jnp.maximum(m_prev, jnp.max(s, axis=0, keepdims=True))
            alpha = jnp.exp2(m_prev - m_new)
            p = jnp.exp2(s - m_new)
            l_ref[hh] = alpha * l_ref[hh] + jnp.sum(p, axis=0, keepdims=True)
            acc_ref[hh] = acc_ref[hh] * alpha + _dot(vt_ref[hh, j], p.astype(BF16))
            m_ref[hh] = m_new

    n_full = (i * tq) // tk

    def full_step(j, carry):
        step(j, False)
        return carry

    lax.fori_loop(0, n_full, full_step, 0)
    step(n_full, True)

    lam = _lambda(lq_ref, lam_init)
    for hh in range(hps):
        o2 = acc_ref[hh] * (1.0 / l_ref[hh])
        ot = o2[:, :tq] - lam * o2[:, tq:]
        o_ref[:, hh * V_HEAD:(hh + 1) * V_HEAD] = _sub_norm(ot.T, g_ref[...], lam_init).astype(BF16)


def _prompt_attention(q, k, v, lq, g, batch, seq, lam_init):
    width = q.shape[1]
    nh = width // V_HEAD
    tq = min(TQ_ATTN, seq)
    tk = min(TK_ATTN, seq)
    hps = math.gcd(HEADS_PER_STEP, nh)
    assert tq & (tq - 1) == 0 and tk % tq == 0 and seq % tk == 0
    q3 = q.reshape(batch, seq, width)
    k3 = k.reshape(batch, seq, width)
    vt = v.reshape(batch, seq // tk, tk, nh, V_HEAD).transpose(0, 3, 1, 4, 2)
    gw = hps * V_HEAD
    out = pl.pallas_call(
        functools.partial(_attn_kernel, tq=tq, tk=tk, hps=hps, lam_init=lam_init),
        grid=(batch, nh // hps, seq // tq),
        in_specs=[pl.BlockSpec((4, D_HEAD), lambda b, h, i: (0, 0)),
                  pl.BlockSpec((1, V_HEAD), lambda b, h, i: (0, 0)),
                  pl.BlockSpec((None, tq, gw), lambda b, h, i: (b, i, h)),
                  pl.BlockSpec((None, seq, gw), lambda b, h, i: (b, 0, h)),
                  pl.BlockSpec((None, hps, seq // tk, V_HEAD, tk), lambda b, h, i: (b, h, 0, 0, 0))],
        out_specs=pl.BlockSpec((None, tq, gw), lambda b, h, i: (b, i, h)),
        out_shape=jax.ShapeDtypeStruct((batch, seq, width), BF16),
        scratch_shapes=[pltpu.VMEM((hps, V_HEAD, 2 * tq), F32),
                        pltpu.VMEM((hps, 1, 2 * tq), F32),
                        pltpu.VMEM((hps, 1, 2 * tq), F32)],
        compiler_params=_params("arbitrary", "arbitrary", "arbitrary"),
        name="prompt_attention",
    )(lq, g, q3, k3, vt)
    return out.reshape(batch * seq, width)


SELF_ROWS = 8


def _paged_kernel(pt_ref, lq_ref, g_ref, q_ref, kn_ref, vn_ref, *rest, ppg, nj, lam_init):
    del pt_ref
    k_refs, v_refs = rest[:ppg], rest[ppg:2 * ppg]
    o_ref, tmp_ref, m_ref, l_ref, acc_ref = rest[2 * ppg:]
    j = pl.program_id(1)
    nh = v_refs[0].shape[1]
    ones = jnp.ones((D_HEAD, LANES), BF16)
    q16 = q_ref[...]

    @pl.when(j == 0)
    def _():
        m_ref[...] = jnp.full(m_ref.shape, NEG_BIG, F32)
        l_ref[...] = jnp.zeros(l_ref.shape, F32)
        acc_ref[...] = jnp.zeros(acc_ref.shape, F32)

    def attend(kp, vp, self_page):
        rows = kp.shape[0]
        ym = (kp * q16[None]).reshape(rows * 2 * nh, D_HEAD).astype(BF16)
        tmp_ref[0:rows * 2 * nh, :] = _dot(ym, ones)
        for c in range(2):
            s = tmp_ref[pl.ds(c, rows * nh, stride=2), :].reshape(rows, nh, LANES)
            if self_page:
                s = jnp.where(lax.broadcasted_iota(jnp.int32, s.shape, 0) == 0, s, NEG_BIG)
            m_prev = m_ref[c]
            m_new = jnp.maximum(m_prev, jnp.max(s, axis=0))
            alpha = jnp.exp2(m_prev - m_new)
            p = jnp.exp2(s - m_new[None])
            l_ref[c] = alpha * l_ref[c] + jnp.sum(p, axis=0)
            acc_ref[c] = alpha * acc_ref[c] + jnp.sum(p * vp, axis=0)
            m_ref[c] = m_new

    for gi in range(ppg):
        attend(k_refs[gi][...], v_refs[gi][...], False)

    @pl.when(j == nj - 1)
    def _():
        first = lax.broadcasted_iota(jnp.int32, (SELF_ROWS, 1, 1), 0) == 0
        attend(jnp.where(first, kn_ref[...][None], 0.0), jnp.where(first, vn_ref[...][None], 0.0), True)
        lam = _lambda(lq_ref, lam_init)
        o = acc_ref[0] * (1.0 / l_ref[0]) - lam * (acc_ref[1] * (1.0 / l_ref[1]))
        o_ref[...] = _sub_norm(o, g_ref[...], lam_init)


def _paged_attention(q, k_new, v_new, cache_k, cache_v, layer, page_table, lq, g, lam_init):
    bs, width = q.shape
    n_pages = page_table.shape[1]
    _, _, page, ncomp, dh = cache_k.shape
    nh, dv = cache_v.shape[3], cache_v.shape[4]
    ppg = math.gcd(PAGES_PER_STEP, n_pages)
    nj = n_pages // ppg

    def page_map(gi):
        return lambda b, j, pt: (layer, pt[b * n_pages + j * ppg + gi], 0, 0, 0)

    def row_spec(r, c):
        return pl.BlockSpec((None, r, c), lambda b, j, pt: (b, 0, 0))

    in_specs = [pl.BlockSpec((4, D_HEAD), lambda b, j, pt: (0, 0)),
                pl.BlockSpec((1, V_HEAD), lambda b, j, pt: (0, 0)),
                row_spec(ncomp, dh), row_spec(ncomp, dh), row_spec(nh, dv)]
    in_specs += [pl.BlockSpec((None, None, page, ncomp, dh), page_map(gi)) for gi in range(ppg)]
    in_specs += [pl.BlockSpec((None, None, page, nh, dv), page_map(gi)) for gi in range(ppg)]
    out = pl.pallas_call(
        functools.partial(_paged_kernel, ppg=ppg, nj=nj, lam_init=lam_init),
        grid_spec=pltpu.PrefetchScalarGridSpec(
            num_scalar_prefetch=1,
            grid=(bs, nj),
            in_specs=in_specs,
            out_specs=row_spec(nh, dv),
            scratch_shapes=[pltpu.VMEM((page * ncomp, LANES), F32),
                            pltpu.VMEM((2, nh, LANES), F32),
                            pltpu.VMEM((2, nh, LANES), F32),
                            pltpu.VMEM((2, nh, dv), F32)]),
        out_shape=jax.ShapeDtypeStruct((bs, nh, dv), F32),
        compiler_params=_params("arbitrary", "arbitrary"),
        name="paged_attention",
    )(page_table.reshape(-1), lq, g, q.astype(F32).reshape(bs, ncomp, dh), k_new.reshape(bs, ncomp, dh),
      v_new.reshape(bs, nh, dv), *([cache_k] * ppg), *([cache_v] * ppg))
    return out.reshape(bs, width)


HALO_ROWS = 32


def _conv_seq_kernel(u_ref, halo_ref, w_ref, b_ref, g_ref, beta_ref, cv_ref, ext_ref, y_ref, *, tpb):
    tm, d = u_ref.shape
    first = (pl.program_id(0) % tpb) == 0
    ext_ref[0:HALO_ROWS, :] = jnp.where(first, 0.0, halo_ref[...])
    ext_ref[HALO_ROWS:, :] = u_ref[...]
    lead = HALO_ROWS - (CONV_K - 1)
    rc = min(CONV_ROWS, tm)

    def chunk(r, carry):
        r0 = pl.multiple_of(r * rc, rc)
        for j in range(d // LANES):
            ls = slice(j * LANES, (j + 1) * LANES)
            blk = ext_ref[pl.ds(r0, rc + HALO_ROWS), ls]
            acc = jnp.zeros((rc, LANES), F32)
            for s in range(8):
                taps = [k for k in range(CONV_K) if (lead + k) % 8 == s]
                if not taps:
                    continue
                span = (lead + taps[-1]) - s + rc
                shifted = blk[s:s + span]
                for k in taps:
                    a = lead + k - s
                    acc = acc + w_ref[k:k + 1, ls] * shifted[a:a + rc]
            y_ref[pl.ds(r0, rc), ls] = acc + b_ref[:, ls]
        return carry

    lax.fori_loop(0, tm // rc, chunk, 0)
    cv_ref[...] = _silu(_layer_norm(y_ref[...], g_ref[...], beta_ref[...])).astype(BF16)


def _conv_seq(rows, u, w, b, g, beta):
    d = u.shape[1]
    tm = rows.tm
    const = lambda i: (0, 0)
    return pl.pallas_call(
        functools.partial(_conv_seq_kernel, tpb=rows.tpb),
        grid=(rows.nt,),
        in_specs=[pl.BlockSpec((tm, d), lambda i: (i, 0)),
                  pl.BlockSpec((HALO_ROWS, d), lambda i: (jnp.maximum(i * (tm // HALO_ROWS) - 1, 0), 0)),
                  pl.BlockSpec((CONV_K, d), const),
                  pl.BlockSpec((1, d), const), pl.BlockSpec((1, d), const), pl.BlockSpec((1, d), const)],
        out_specs=pl.BlockSpec((tm, d), lambda i: (i, 0)),
        out_shape=jax.ShapeDtypeStruct(u.shape, BF16),
        scratch_shapes=[pltpu.VMEM((tm + HALO_ROWS, d), F32), pltpu.VMEM((tm, d), F32)],
        compiler_params=_params("arbitrary"),
        name="conv_seq",
    )(u, u, w, b, g, beta)


def _conv_step_kernel(st_ref, u_ref, w_ref, b_ref, g_ref, beta_ref, cv_ref, acc_ref):
    k = pl.program_id(1)
    wk = w_ref[pl.ds(k, 1), :]

    @pl.when(k == 0)
    def _():
        acc_ref[...] = jnp.zeros(acc_ref.shape, F32)

    @pl.when(k < CONV_K - 1)
    def _():
        acc_ref[...] += st_ref[...] * wk

    @pl.when(k == CONV_K - 1)
    def _():
        y = acc_ref[...] + u_ref[...] * wk + b_ref[...]
        cv_ref[...] = _silu(_layer_norm(y, g_ref[...], beta_ref[...])).astype(BF16)


def _conv_step(rows, state, u, w, b, g, beta):
    d = u.shape[1]
    tm = rows.tm
    const = lambda i, k: (0, 0)
    row_spec = pl.BlockSpec((tm, d), lambda i, k: (i, 0))
    return pl.pallas_call(
        _conv_step_kernel,
        grid=(rows.nt, CONV_K),
        in_specs=[pl.BlockSpec((tm, d), lambda i, k: (i, jnp.minimum(k, CONV_K - 2))),
                  row_spec,
                  pl.BlockSpec((CONV_K, d), const),
                  pl.BlockSpec((1, d), const), pl.BlockSpec((1, d), const), pl.BlockSpec((1, d), const)],
        out_specs=row_spec,
        out_shape=jax.ShapeDtypeStruct(u.shape, BF16),
        scratch_shapes=[pltpu.VMEM((tm, d), F32)],
        compiler_params=_params("arbitrary", "arbitrary"),
        name="conv_step",
    )(state.reshape(state.shape[0], (CONV_K - 1) * d), u, w, b, g, beta)


def _mix_kernel(cv_ref, ga_ref, gb_ref, at_ref, x_ref, g1_ref, sc2_ref, sh2_ref, wp_ref, bp_ref, wo_ref,
                lg_ref, lb_ref, x1_ref, h2_ref, wpbf_ref, wobf_ref, *, alpha):
    @pl.when(pl.program_id(0) == 0)
    def _():
        wpbf_ref[...] = wp_ref[...].astype(BF16)
        wobf_ref[...] = wo_ref[...].astype(BF16)

    conv_out = _dot(cv_ref[...], wpbf_ref[...]) + bp_ref[...]
    merged = ga_ref[...].astype(F32) * conv_out + gb_ref[...].astype(F32) * at_ref[...].astype(F32)
    mix = _dot(merged.astype(BF16), wobf_ref[...])
    x1 = _layer_norm(alpha * x_ref[...] + g1_ref[...] * mix, lg_ref[...], lb_ref[...])
    x1_ref[...] = x1
    h2_ref[...] = (x1 * (1.0 + sc2_ref[...]) + sh2_ref[...]).astype(BF16)


def _mix(rows, cv, gates, attn, x, g1, sc2, sh2, w_pw2, b_pw2, w_o, ln_g, ln_b, alpha):
    d = x.shape[1]
    tm = rows.tm
    row_spec = pl.BlockSpec((tm, d), lambda i: (i, 0))
    const = lambda i: (0, 0)
    vec = pl.BlockSpec((1, d), const)
    mat = pl.BlockSpec((d, d), const)
    mod = rows.mod_spec(d, 1)
    return pl.pallas_call(
        functools.partial(_mix_kernel, alpha=alpha),
        grid=(rows.nt,),
        in_specs=[row_spec, row_spec, pl.BlockSpec((tm, d), lambda i: (i, 1)), row_spec, row_spec,
                  mod, mod, mod, mat, vec, mat, vec, vec],
        out_specs=[row_spec, row_spec],
        out_shape=[jax.ShapeDtypeStruct(x.shape, F32), jax.ShapeDtypeStruct(x.shape, BF16)],
        scratch_shapes=[pltpu.VMEM((d, d), BF16), pltpu.VMEM((d, d), BF16)],
        compiler_params=_params("arbitrary"),
        name="mix",
    )(cv, gates, gates, attn, x, g1, sc2, sh2, w_pw2, b_pw2, w_o, ln_g, ln_b)


FFN_HALO_ROWS = 8


def _ffn_in_kernel(h_ref, wg_ref, wu_ref, gt_ref, up_ref, *rest, tail):
    h = h_ref[...]
    gt = _dot(h, wg_ref[...])
    gt_ref[...] = gt.astype(gt_ref.dtype)
    up_ref[...] = _dot(h, wu_ref[...]).astype(BF16)
    if tail:
        rest[0][...] = gt[gt.shape[0] - tail:, :]


def _ffn_in(rows, h, w_ffn_in_bf, layer, ffn, gt_dtype, tail):
    d = h.shape[1]
    tm = rows.tm
    tn = ffn // 2 if (ffn // 2) % LANES == 0 else ffn
    nj = ffn // tn
    o_spec = pl.BlockSpec((tm, tn), lambda j, i: (i, j))
    out_specs = [o_spec, o_spec]
    out_shape = [jax.ShapeDtypeStruct((rows.m, ffn), gt_dtype), jax.ShapeDtypeStruct((rows.m, ffn), BF16)]
    if tail:
        out_specs.append(pl.BlockSpec((None, tail, tn), lambda j, i: (i // rows.tpb, 0, j)))
        out_shape.append(jax.ShapeDtypeStruct((rows.m // rows.seq, tail, ffn), F32))
    return pl.pallas_call(
        functools.partial(_ffn_in_kernel, tail=tail),
        grid=(nj, rows.nt),
        in_specs=[pl.BlockSpec((tm, d), lambda j, i: (i, 0)),
                  pl.BlockSpec((None, d, tn), lambda j, i: (layer, 0, j)),
                  pl.BlockSpec((None, d, tn), lambda j, i: (layer, 0, nj + j))],
        out_specs=out_specs,
        out_shape=out_shape,
        compiler_params=_params("arbitrary", "arbitrary"),
        name="ffn_in",
    )(h, w_ffn_in_bf, w_ffn_in_bf)


def _ffn_tail(act, x_ref, g2_ref, wd_ref, lg_ref, lb_ref, x2_ref, nxt, alpha):
    f = _dot(act.astype(BF16), wd_ref[...])
    x2 = _layer_norm(alpha * x_ref[...] + g2_ref[...] * f, lg_ref[...], lb_ref[...])
    x2_ref[...] = x2
    if nxt:
        sc_ref, sh_ref, hn_ref = nxt
        hn_ref[...] = (x2 * (1.0 + sc_ref[...]) + sh_ref[...]).astype(BF16)


def _ffn_down_seq_kernel(gt_ref, halo_ref, up_ref, x_ref, g2_ref, cw_ref, cb_ref, wd_ref, lg_ref, lb_ref,
                         *rest, tpb, alpha, has_next):
    if has_next:
        sc_ref, sh_ref, x2_ref, hn_ref, ext_ref = rest
        nxt = (sc_ref, sh_ref, hn_ref)
    else:
        x2_ref, ext_ref = rest
        nxt = None
    tm = gt_ref.shape[0]
    first = (pl.program_id(0) % tpb) == 0
    ext_ref[0:FFN_HALO_ROWS, :] = jnp.where(first, 0.0, halo_ref[...].astype(F32))
    ext_ref[FFN_HALO_ROWS:, :] = gt_ref[...].astype(F32)
    lead = FFN_HALO_ROWS - (FFN_CONV_K - 1)
    conv = cb_ref[...]
    for k in range(FFN_CONV_K):
        conv = conv + cw_ref[k:k + 1, :] * ext_ref[lead + k:lead + k + tm, :]
    _ffn_tail(_silu(conv) * up_ref[...].astype(F32), x_ref, g2_ref, wd_ref, lg_ref, lb_ref, x2_ref, nxt, alpha)


def _ffn_down_step_kernel(gt_ref, p2_ref, p1_ref, up_ref, x_ref, g2_ref, cw_ref, cb_ref, wd_ref, lg_ref, lb_ref,
                          *rest, alpha, has_next):
    if has_next:
        sc_ref, sh_ref, x2_ref, hn_ref = rest
        nxt = (sc_ref, sh_ref, hn_ref)
    else:
        (x2_ref,) = rest
        nxt = None
    conv = (cb_ref[...] + cw_ref[0:1, :] * p2_ref[...] + cw_ref[1:2, :] * p1_ref[...]
            + cw_ref[2:3, :] * gt_ref[...].astype(F32))
    _ffn_tail(_silu(conv) * up_ref[...].astype(F32), x_ref, g2_ref, wd_ref, lg_ref, lb_ref, x2_ref, nxt, alpha)


def _ffn_down(rows, gt, up, x, g2, conv_w, conv_b, w_down_bf, ln_g, ln_b, alpha, nxt, prev):
    m, d = x.shape
    ffn = gt.shape[1]
    tm = min(TM_FFN_DOWN, rows.tm)
    sub = rows.tm // tm
    tpb = rows.tpb * sub if rows.mod_rows == 1 else 1
    row_d = pl.BlockSpec((tm, d), lambda i: (i, 0))
    row_f = pl.BlockSpec((tm, ffn), lambda i: (i, 0))
    const = lambda i: (0, 0)
    vec_d = pl.BlockSpec((1, d), const)
    if rows.mod_rows == 1:
        mod = pl.BlockSpec((None, 1, d), lambda i: (i // tpb, 0, 0))
        mods = lambda a: a
    else:
        mod = pl.BlockSpec((tm, d), lambda i: (i, 0))
        mods = lambda a: a.reshape(m, d)
    common_in = [row_f, row_d, mod, pl.BlockSpec((FFN_CONV_K, ffn), const), pl.BlockSpec((1, ffn), const),
                 pl.BlockSpec((ffn, d), const), vec_d, vec_d]
    common_args = [up, x, mods(g2), conv_w, conv_b, w_down_bf, ln_g, ln_b]
    out_specs = [row_d]
    out_shape = [jax.ShapeDtypeStruct((m, d), F32)]
    if nxt:
        common_in += [mod, mod]
        common_args += [mods(nxt[0]), mods(nxt[1])]
        out_specs.append(row_d)
        out_shape.append(jax.ShapeDtypeStruct((m, d), BF16))
    if prev is None:
        body = functools.partial(_ffn_down_seq_kernel, tpb=tpb, alpha=alpha, has_next=bool(nxt))
        lead_in = [row_f, pl.BlockSpec((FFN_HALO_ROWS, ffn),
                                       lambda i: (jnp.maximum(i * (tm // FFN_HALO_ROWS) - 1, 0), 0))]
        lead_args = [gt, gt]
        scratch = [pltpu.VMEM((tm + FFN_HALO_ROWS, ffn), F32)]
    else:
        body = functools.partial(_ffn_down_step_kernel, alpha=alpha, has_next=bool(nxt))
        lead_in = [row_f, row_f, row_f]
        lead_args = [gt, prev[0], prev[1]]
        scratch = []
    res = pl.pallas_call(
        body,
        grid=(m // tm,),
        in_specs=lead_in + common_in,
        out_specs=out_specs,
        out_shape=out_shape,
        scratch_shapes=scratch,
        compiler_params=_params("arbitrary"),
        name="ffn_down",
    )(*lead_args, *common_args)
    return (res[0], res[1]) if nxt else (res[0], None)


def _rope_tables(pos):
    half = ROT_DIM // 2
    inv = ROPE_THETA ** (-jnp.arange(half, dtype=F32) / half)
    ang = pos.astype(F32)[:, None] * inv[None, :]
    cos, sin = jnp.cos(ang), jnp.sin(ang)
    t = pos.shape[0]
    zeros = jnp.zeros((t, half), F32)
    rest = D_HEAD - ROT_DIM
    c = jnp.concatenate([cos, cos, jnp.ones((t, rest), F32)], axis=1)
    sa = jnp.concatenate([-sin, zeros, jnp.zeros((t, rest), F32)], axis=1)
    sb = jnp.concatenate([zeros, sin, jnp.zeros((t, rest), F32)], axis=1)
    reps = LANES // D_HEAD
    return tuple(jnp.tile(a, (1, reps)) for a in (c, sa, sb))


def kernel(x_prompt, x_sample, cache_k, cache_v, state_conv, state_ffn_conv, page_table, c_prompt, c_sample,
           w_ada, b_ada, w_in, lambda_qk, attn_subln_g, glu_b, dw_w, dw_b, conv_ln_g, conv_ln_b, w_pw2, b_pw2,
           w_o, ln1_g, ln1_b, w_ffn_in, ffn_dw_w, ffn_dw_b, w_down, ln2_g, ln2_b):
    depth, d, _ = w_in.shape
    bp, seq, _ = x_prompt.shape
    bs, dec_seq, _ = x_sample.shape
    assert dec_seq == 1
    n_heads = cache_v.shape[3]
    qk_width = cache_k.shape[3] * cache_k.shape[4]
    v_width = n_heads * cache_v.shape[4]
    conv_ch = state_conv.shape[-1]
    ffn = state_ffn_conv.shape[-1]
    assert cache_k.shape[4] == D_HEAD and cache_v.shape[4] == V_HEAD and qk_width == v_width
    assert state_conv.shape[2] == CONV_K - 1 and state_ffn_conv.shape[2] == FFN_CONV_K - 1
    past = page_table.shape[1] * cache_k.shape[2]
    alpha = (2 * depth) ** 0.25
    q_scale = (D_HEAD ** -0.5) * LOG2E

    rows_p = _Rows(bp * seq, min(TM_PROMPT, seq), seq)
    rows_s = _Rows(bs, bs, 1)

    n_c = bp + bs
    pad = (-n_c) % 16
    c_all = jnp.concatenate([c_prompt, c_sample, jnp.zeros((pad, d), F32)], axis=0)
    ada = _ada(c_all, w_ada, b_ada)

    def mods(layer, rows, lo, hi):
        return [rows.mod(a) for a in jnp.split(ada[layer, lo:hi], 6, axis=-1)]

    mods_p = [mods(l, rows_p, 0, bp) for l in range(depth)]
    mods_s = [mods(l, rows_s, bp, n_c) for l in range(depth)]

    tables_p = _rope_tables(jnp.arange(seq))
    tables_s = _rope_tables(jnp.tile(past + jnp.arange(dec_seq), bs))

    w_ffn_in_bf = w_ffn_in.astype(BF16)
    w_down_bf = w_down.astype(BF16)
    row2 = lambda a: a.reshape(a.shape[0], 1, a.shape[1])

    def layer(l, rows, x, h, mod, nxt, tables, sample):
        sh1, sc1, g1, sh2, sc2, g2 = mod
        del sh1, sc1
        q = _proj_q(rows, h, w_in, l, tables, qk_width, q_scale)
        k, k_bf = _proj_kv(rows, h, w_in, l, qk_width, qk_width, tables)
        v, v_bf = _proj_kv(rows, h, w_in, l, 2 * qk_width, v_width, None)
        col = 2 * qk_width + v_width
        lam_init = 0.8 - 0.6 * math.exp(-0.3 * l)
        lq, sub_g = lambda_qk[l], attn_subln_g[l].reshape(1, V_HEAD)
        conv_args = (dw_w[l], row2(dw_b)[l], row2(conv_ln_g)[l], row2(conv_ln_b)[l])
        if sample:
            (u,) = _proj_glu(rows, h, w_in, glu_b, l, col, conv_ch, 0)
            attn = _paged_attention(q, k, v, cache_k, cache_v, l, page_table, lq, sub_g, lam_init)
            cv = _conv_step(rows, state_conv[l], u, *conv_args)
            conv_hist = jnp.concatenate([state_conv[l][:, 1:], u[:, None, :]], axis=1)
        else:
            u, u_tail = _proj_glu(rows, h, w_in, glu_b, l, col, conv_ch, HALO_ROWS)
            attn = _prompt_attention(q, k_bf, v_bf, lq, sub_g, bp, seq, lam_init)
            cv = _conv_seq(rows, u, *conv_args)
            conv_hist = u_tail[:, HALO_ROWS - (CONV_K - 1):]
        gates = _proj_gate(rows, h, w_in, l, col + 2 * conv_ch, 2 * d)
        x1, h2 = _mix(rows, cv, gates, attn, x, g1, sc2, sh2, w_pw2[l], row2(b_pw2)[l], w_o[l],
                      row2(ln1_g)[l], row2(ln1_b)[l], alpha)
        ffn_args = (ffn_dw_w[l], row2(ffn_dw_b)[l], w_down_bf[l], row2(ln2_g)[l], row2(ln2_b)[l], alpha, nxt)
        if sample:
            gt, up = _ffn_in(rows, h2, w_ffn_in_bf, l, ffn, F32, 0)
            hist = state_ffn_conv[l]
            x2, h_next = _ffn_down(rows, gt, up, x1, g2, *ffn_args, (hist[:, 0], hist[:, 1]))
            ffn_hist = jnp.concatenate([hist[:, 1:], gt[:, None, :]], axis=1)
        else:
            gt, up, gt_tail = _ffn_in(rows, h2, w_ffn_in_bf, l, ffn, BF16, FFN_HALO_ROWS)
            x2, h_next = _ffn_down(rows, gt, up, x1, g2, *ffn_args, None)
            ffn_hist = gt_tail[:, FFN_HALO_ROWS - (FFN_CONV_K - 1):]
        return x2, h_next, k, v, conv_hist, ffn_hist

    xp = x_prompt.reshape(bp * seq, d)
    xs = x_sample.reshape(bs, d)
    hp = _modulate(rows_p, xp, mods_p[0][1], mods_p[0][0])
    hs = _modulate(rows_s, xs, mods_s[0][1], mods_s[0][0])
    outs_p, outs_s = [], []
    for l in range(depth):
        nxt_p = (mods_p[l + 1][1], mods_p[l + 1][0]) if l + 1 < depth else None
        nxt_s = (mods_s[l + 1][1], mods_s[l + 1][0]) if l + 1 < depth else None
        xp, hp, *rest = layer(l, rows_p, xp, hp, mods_p[l], nxt_p, tables_p, False)
        outs_p.append(rest)
        xs, hs, *rest = layer(l, rows_s, xs, hs, mods_s[l], nxt_s, tables_s, True)
        outs_s.append(rest)

    def stack(outs, i, shape):
        return jnp.stack([o[i].reshape(shape) for o in outs])

    kd = cache_k.shape[3]
    return (xp.reshape(bp, seq, d), xs.reshape(bs, dec_seq, d),
            stack(outs_p, 0, (bp, seq, kd, D_HEAD)), stack(outs_p, 1, (bp, seq, n_heads, V_HEAD)),
            stack(outs_p, 2, (bp, CONV_K - 1, conv_ch)), stack(outs_p, 3, (bp, FFN_CONV_K - 1, ffn)),
            stack(outs_s, 0, (bs, dec_seq, kd, D_HEAD)), stack(outs_s, 1, (bs, dec_seq, n_heads, V_HEAD)),
            stack(outs_s, 2, (bs, CONV_K - 1, conv_ch)), stack(outs_s, 3, (bs, FFN_CONV_K - 1, ffn)))
```

```python
import functools
import math

import jax
import jax.numpy as jnp
from jax import lax
from jax.experimental import pallas as pl
from jax.experimental.pallas import tpu as pltpu

BF16 = jnp.bfloat16
F32 = jnp.float32

D_HEAD = 64
V_HEAD = 2 * D_HEAD
ROT_DIM = D_HEAD // 4
ROPE_THETA = 500000.0
CONV_K = 31
FFN_CONV_K = 3
LN_EPS = 1e-5
LANES = 128
NEG_BIG = -1e30
LOG2E = 1.4426950408889634
VMEM_LIMIT_BYTES = 56 * 1024 * 1024

TM_PROMPT = 1024
TN_PROJ = 1024
TM_FFN_DOWN = 512
TQ_ATTN = 256
TK_ATTN = 512
CONV_ROWS = 64
PAGES_PER_STEP = 16
HEADS_PER_STEP = 4
QK_AHEAD = 2


def _params(*sem):
    return pltpu.CompilerParams(dimension_semantics=sem, vmem_limit_bytes=VMEM_LIMIT_BYTES)


def _silu(z):
    return z * jax.nn.sigmoid(z)


def _layer_norm(y, g, b):
    mu = jnp.mean(y, axis=-1, keepdims=True)
    d = y - mu
    var = jnp.mean(d * d, axis=-1, keepdims=True)
    return d * lax.rsqrt(var + LN_EPS) * g + b


def _dot(a, b):
    return jnp.dot(a, b, preferred_element_type=F32)


def _ada_kernel(c_ref, w_ref, b_ref, o_ref):
    s = _silu(c_ref[...]).astype(BF16)
    o_ref[...] = _dot(s, w_ref[...].astype(BF16)) + b_ref[...]


def _ada(c_all, w_ada, b_ada):
    nl, d, n = w_ada.shape
    mc = c_all.shape[0]
    tn = 1536 if n % 1536 == 0 else n
    return pl.pallas_call(
        _ada_kernel,
        grid=(nl, n // tn),
        in_specs=[pl.BlockSpec((mc, d), lambda l, j: (0, 0)),
                  pl.BlockSpec((None, d, tn), lambda l, j: (l, 0, j)),
                  pl.BlockSpec((None, 1, tn), lambda l, j: (l, 0, j))],
        out_specs=pl.BlockSpec((None, mc, tn), lambda l, j: (l, 0, j)),
        out_shape=jax.ShapeDtypeStruct((nl, mc, n), F32),
        compiler_params=_params("arbitrary", "arbitrary"),
        name="ada",
    )(c_all, w_ada, b_ada.reshape(nl, 1, n))


class _Rows:
    def __init__(self, m, tm, seq):
        self.m, self.tm, self.seq = m, tm, seq
        self.nt = m // tm
        if seq >= tm:
            assert seq % tm == 0
            self.tpb, self.mod_rows = seq // tm, 1
        else:
            assert seq == 1
            self.tpb, self.mod_rows = 1, tm

    def mod(self, a):
        return a.reshape(-1, self.mod_rows, a.shape[-1])

    def mod_spec(self, d, grid_rank):
        if grid_rank == 1:
            return pl.BlockSpec((None, self.mod_rows, d), lambda i: (i // self.tpb, 0, 0))
        return pl.BlockSpec((None, self.mod_rows, d), lambda j, i: (i // self.tpb, 0, 0))


def _modulate_kernel(x_ref, sc_ref, sh_ref, h_ref):
    h_ref[...] = (x_ref[...] * (1.0 + sc_ref[...]) + sh_ref[...]).astype(BF16)


def _modulate(rows, x, sc, sh):
    d = x.shape[1]
    row_spec = pl.BlockSpec((rows.tm, d), lambda i: (i, 0))
    return pl.pallas_call(
        _modulate_kernel,
        grid=(rows.nt,),
        in_specs=[row_spec, rows.mod_spec(d, 1), rows.mod_spec(d, 1)],
        out_specs=row_spec,
        out_shape=jax.ShapeDtypeStruct(x.shape, BF16),
        compiler_params=_params("arbitrary"),
        name="modulate",
    )(x, sc, sh)


def _load_weight(w_ref, wbf_ref):
    @pl.when(pl.program_id(1) == 0)
    def _():
        wbf_ref[...] = w_ref[...].astype(BF16)


def _rope_chunk(y, c, sa, sb):
    return y * c + pltpu.roll(y, LANES - ROT_DIM // 2, 1) * sa + pltpu.roll(y, ROT_DIM // 2, 1) * sb


def _q_kernel(h_ref, w_ref, c_ref, sa_ref, sb_ref, q_ref, wbf_ref, *, scale):
    _load_weight(w_ref, wbf_ref)
    y = _dot(h_ref[...], wbf_ref[...])
    c, sa, sb = c_ref[...], sa_ref[...], sb_ref[...]
    for j in range(y.shape[1] // LANES):
        ls = slice(j * LANES, (j + 1) * LANES)
        q_ref[:, ls] = (_rope_chunk(y[:, ls], c, sa, sb) * scale).astype(BF16)


def _kv_kernel(h_ref, w_ref, *rest, rope):
    if rope:
        c_ref, sa_ref, sb_ref, o_ref, obf_ref, wbf_ref = rest
    else:
        o_ref, obf_ref, wbf_ref = rest
    _load_weight(w_ref, wbf_ref)
    y = _dot(h_ref[...], wbf_ref[...])
    if rope:
        c, sa, sb = c_ref[...], sa_ref[...], sb_ref[...]
        for j in range(y.shape[1] // LANES):
            ls = slice(j * LANES, (j + 1) * LANES)
            r = _rope_chunk(y[:, ls], c, sa, sb)
            o_ref[:, ls] = r
            obf_ref[:, ls] = r.astype(BF16)
    else:
        o_ref[...] = y
        obf_ref[...] = y.astype(BF16)


def _glu_kernel(h_ref, wa_ref, wb_ref, ba_ref, bb_ref, u_ref, *rest, tail):
    if tail:
        tail_ref, wabf_ref, wbbf_ref = rest
    else:
        wabf_ref, wbbf_ref = rest
    _load_weight(wa_ref, wabf_ref)
    _load_weight(wb_ref, wbbf_ref)
    h = h_ref[...]
    a = _dot(h, wabf_ref[...]) + ba_ref[...]
    b = _dot(h, wbbf_ref[...]) + bb_ref[...]
    u = a * jax.nn.sigmoid(b)
    u_ref[...] = u
    if tail:
        tail_ref[...] = u[u.shape[0] - tail:, :]


def _gate_kernel(h_ref, w_ref, o_ref, wbf_ref):
    _load_weight(w_ref, wbf_ref)
    o_ref[...] = jax.nn.sigmoid(_dot(h_ref[...], wbf_ref[...])).astype(BF16)


def _proj_specs(rows, d, tn, col0, layer):
    h_spec = pl.BlockSpec((rows.tm, d), lambda j, i: (i, 0))
    w_spec = pl.BlockSpec((None, d, tn), lambda j, i: (layer, 0, col0 // tn + j))
    o_spec = pl.BlockSpec((rows.tm, tn), lambda j, i: (i, j))
    return h_spec, w_spec, o_spec


def _table_specs(rows, tables):
    nt = tables[0].shape[0] // rows.tm
    return [pl.BlockSpec((rows.tm, LANES), lambda j, i: (i % nt, 0))] * 3


def _proj_q(rows, h, w_in, layer, tables, width, scale):
    d = h.shape[1]
    tn = min(TN_PROJ, width)
    h_spec, w_spec, o_spec = _proj_specs(rows, d, tn, 0, layer)
    return pl.pallas_call(
        functools.partial(_q_kernel, scale=scale),
        grid=(width // tn, rows.nt),
        in_specs=[h_spec, w_spec] + _table_specs(rows, tables),
        out_specs=o_spec,
        out_shape=jax.ShapeDtypeStruct((rows.m, width), BF16),
        scratch_shapes=[pltpu.VMEM((d, tn), BF16)],
        compiler_params=_params("arbitrary", "arbitrary"),
        name="proj_q",
    )(h, w_in, *tables)


def _proj_kv(rows, h, w_in, layer, col0, width, tables):
    d = h.shape[1]
    tn = min(TN_PROJ, width)
    h_spec, w_spec, o_spec = _proj_specs(rows, d, tn, col0, layer)
    rope = tables is not None
    extra = _table_specs(rows, tables) if rope else []
    args = tuple(tables) if rope else ()
    return pl.pallas_call(
        functools.partial(_kv_kernel, rope=rope),
        grid=(width // tn, rows.nt),
        in_specs=[h_spec, w_spec] + extra,
        out_specs=[o_spec, o_spec],
        out_shape=[jax.ShapeDtypeStruct((rows.m, width), F32),
                   jax.ShapeDtypeStruct((rows.m, width), BF16)],
        scratch_shapes=[pltpu.VMEM((d, tn), BF16)],
        compiler_params=_params("arbitrary", "arbitrary"),
        name="proj_k" if rope else "proj_v",
    )(h, w_in, *args)


def _proj_glu(rows, h, w_in, glu_b, layer, col0, width, tail):
    d = h.shape[1]
    tn = min(TN_PROJ, width)
    nl = glu_b.shape[0]
    h_spec, wa_spec, o_spec = _proj_specs(rows, d, tn, col0, layer)
    _, wb_spec, _ = _proj_specs(rows, d, tn, col0 + width, layer)
    ba_spec = pl.BlockSpec((None, 1, tn), lambda j, i: (layer, 0, j))
    bb_spec = pl.BlockSpec((None, 1, tn), lambda j, i: (layer, 0, width // tn + j))
    out_specs = [o_spec]
    out_shape = [jax.ShapeDtypeStruct((rows.m, width), F32)]
    if tail:
        out_specs.append(pl.BlockSpec((None, tail, tn), lambda j, i: (i // rows.tpb, 0, j)))
        out_shape.append(jax.ShapeDtypeStruct((rows.m // rows.seq, tail, width), F32))
    gb = glu_b.reshape(nl, 1, 2 * width)
    return pl.pallas_call(
        functools.partial(_glu_kernel, tail=tail),
        grid=(width // tn, rows.nt),
        in_specs=[h_spec, wa_spec, wb_spec, ba_spec, bb_spec],
        out_specs=out_specs,
        out_shape=out_shape,
        scratch_shapes=[pltpu.VMEM((d, tn), BF16), pltpu.VMEM((d, tn), BF16)],
        compiler_params=_params("arbitrary", "arbitrary"),
        name="proj_glu",
    )(h, w_in, w_in, gb, gb)


def _proj_gate(rows, h, w_in, layer, col0, width):
    d = h.shape[1]
    tn = min(TN_PROJ, width)
    h_spec, w_spec, o_spec = _proj_specs(rows, d, tn, col0, layer)
    return pl.pallas_call(
        _gate_kernel,
        grid=(width // tn, rows.nt),
        in_specs=[h_spec, w_spec],
        out_specs=o_spec,
        out_shape=jax.ShapeDtypeStruct((rows.m, width), BF16),
        scratch_shapes=[pltpu.VMEM((d, tn), BF16)],
        compiler_params=_params("arbitrary", "arbitrary"),
        name="proj_gate",
    )(h, w_in)


def _lambda(lq_ref, lam_init):
    lq = lq_ref[...]
    a = jnp.sum(lq[0:1] * lq[1:2], axis=1, keepdims=True)
    b = jnp.sum(lq[2:3] * lq[3:4], axis=1, keepdims=True)
    return jnp.exp(a) - jnp.exp(b) + lam_init


def _sub_norm(o, g, lam_init):
    ms = jnp.mean(o * o, axis=-1, keepdims=True)
    return o * lax.rsqrt(ms + LN_EPS) * g * (1.0 - lam_init)


def _attn_kernel(lq_ref, g_ref, q_ref, k_ref, vt_ref, o_ref, acc_ref, m_ref, l_ref, *, tq, tk, hps, lam_init):
    i = pl.program_id(2)
    q2 = []
    for hh in range(hps):
        q = q_ref[:, hh * V_HEAD:(hh + 1) * V_HEAD]
        lane = lax.broadcasted_iota(jnp.int32, q.shape, 1)
        zero = jnp.zeros_like(q)
        q2.append(jnp.concatenate([jnp.where(lane < D_HEAD, q, zero), jnp.where(lane >= D_HEAD, q, zero)],
                                  axis=0))
    m_ref[...] = jnp.full(m_ref.shape, NEG_BIG, F32)
    l_ref[...] = jnp.zeros(l_ref.shape, F32)
    acc_ref[...] = jnp.zeros(acc_ref.shape, F32)

    def step(j, masked, part=None):
        pi, pn = part if part else (0, 1)
        nk = tk // pn
        k0 = pl.multiple_of(j * tk + pi * nk, nk)

        def qk(hh):
            kt = k_ref[pl.ds(k0, nk), hh * V_HEAD:(hh + 1) * V_HEAD]
            return lax.dot_general(kt, q2[hh], (((1,), (1,)), ((), ())), preferred_element_type=F32)

        scores = [qk(hh) for hh in range(min(QK_AHEAD, hps))]
        for hh in range(hps):
            if hh + QK_AHEAD < hps:
                scores.append(qk(hh + QK_AHEAD))
            s = scores[hh]
            if masked:
                kpos = k0 + lax.broadcasted_iota(jnp.int32, s.shape, 0)
                qpos = i * tq + (lax.broadcasted_iota(jnp.int32, s.shape, 1) & (tq - 1))
                s = jnp.where(kpos <= qpos, s, NEG_BIG)
            m_prev = m_ref[hh]
            m_new = jnp.maximum(m_prev, jnp.max(s, axis=0, keepdims=True))
            alpha = jnp.exp2(m_prev - m_new)
            p = jnp.exp2(s - m_new)
            l_ref[hh] = alpha * l_ref[hh] + jnp.sum(p, axis=0, keepdims=True)
            vt = vt_ref[hh, j, :, pi * nk:(pi + 1) * nk]
            acc_ref[hh] = acc_ref[hh] * alpha + _dot(vt, p.astype(BF16))
            m_ref[hh] = m_new

    n_full = (i * tq) // tk

    def full_step(j, carry):
        step(j, False)
        return carry

    lax.fori_loop(0, n_full, full_step, 0)
    parts = tk // tq
    diag = i % parts
    for pi in range(parts):
        if pi < parts - 1:
            @pl.when(pi < diag)
            def _(pi=pi):
                step(n_full, False, (pi, parts))

        @pl.when(pi == diag)
        def _(pi=pi):
            step(n_full, True, (pi, parts))

    lam = _lambda(lq_ref, lam_init)
    for hh in range(hps):
        o2 = acc_ref[hh] * (1.0 / l_ref[hh])
        ot = o2[:, :tq] - lam * o2[:, tq:]
        o_ref[:, hh * V_HEAD:(hh + 1) * V_HEAD] = _sub_norm(ot.T, g_ref[...], lam_init).astype(BF16)


def _prompt_attention(q, k, v, lq, g, batch, seq, lam_init):
    width = q.shape[1]
    nh = width // V_HEAD
    tq = min(TQ_ATTN, seq)
    tk = min(TK_ATTN, seq)
    hps = math.gcd(HEADS_PER_STEP, nh)
    assert tq & (tq - 1) == 0 and tk % tq == 0 and seq % tk == 0
    q3 = q.reshape(batch, seq, width)
    k3 = k.reshape(batch, seq, width)
    vt = v.reshape(batch, seq // tk, tk, nh, V_HEAD).transpose(0, 3, 1, 4, 2)
    gw = hps * V_HEAD
    out = pl.pallas_call(
        functools.partial(_attn_kernel, tq=tq, tk=tk, hps=hps, lam_init=lam_init),
        grid=(batch, nh // hps, seq // tq),
        in_specs=[pl.BlockSpec((4, D_HEAD), lambda b, h, i: (0, 0)),
                  pl.BlockSpec((1, V_HEAD), lambda b, h, i: (0, 0)),
                  pl.BlockSpec((None, tq, gw), lambda b, h, i: (b, i, h)),
                  pl.BlockSpec((None, seq, gw), lambda b, h, i: (b, 0, h)),
                  pl.BlockSpec((None, hps, seq // tk, V_HEAD, tk), lambda b, h, i: (b, h, 0, 0, 0))],
        out_specs=pl.BlockSpec((None, tq, gw), lambda b, h, i: (b, i, h)),
        out_shape=jax.ShapeDtypeStruct((batch, seq, width), BF16),
        scratch_shapes=[pltpu.VMEM((hps, V_HEAD, 2 * tq), F32),
                        pltpu.VMEM((hps, 1, 2 * tq), F32),
                        pltpu.VMEM((hps, 1, 2 * tq), F32)],
        compiler_params=_params("arbitrary", "arbitrary", "arbitrary"),
        name="prompt_attention",
    )(lq, g, q3, k3, vt)
    return out.reshape(batch * seq, width)


def _paged_kernel(pt_ref, lq_ref, g_ref, qcol_ref, q_ref, kn_ref, vn_ref, *rest, ppg, nj, lam_init):
    del pt_ref
    k_refs, v_refs = rest[:ppg], rest[ppg:2 * ppg]
    o_ref, s_ref, d_ref, acc_ref = rest[2 * ppg:]
    ph, j = pl.program_id(1), pl.program_id(2)
    ncomp, dh, page = k_refs[0].shape
    nh, dv = v_refs[0].shape[1], v_refs[0].shape[2]
    n_past = nj * ppg
    assert nh & (nh - 1) == 0

    @pl.when(ph == 0)
    def _():
        qcol = jnp.broadcast_to(qcol_ref[...], (ncomp * dh, page)).reshape(ncomp, dh, page)
        for gi in range(ppg):
            pg = j * ppg + gi
            s_ref[pl.ds(pl.multiple_of(pg * ncomp, ncomp), ncomp), :] = jnp.sum(k_refs[gi][...] * qcol, axis=1)

    @pl.when((ph == 0) & (j == nj - 1))
    def _():
        s_self = jnp.sum(q_ref[...] * kn_ref[...], axis=1, keepdims=True)
        lane = lax.broadcasted_iota(jnp.int32, (ncomp, page), 1)
        s_ref[n_past * ncomp:(n_past + 1) * ncomp, :] = jnp.where(lane == 0, s_self, NEG_BIG)
        s = s_ref[...].reshape(n_past + 1, ncomp, page)
        m = jnp.max(jnp.max(s, axis=0), axis=1, keepdims=True)
        p = jnp.exp2(s - m[None])
        inv = 1.0 / jnp.sum(jnp.sum(p, axis=0), axis=1, keepdims=True)
        s_ref[...] = (p * inv[None]).reshape((n_past + 1) * ncomp, page)
        lam = _lambda(lq_ref, lam_init)
        rows = (n_past + 1) * nh
        d_ref[...] = s_ref[pl.ds(0, rows, stride=2), :] - lam * s_ref[pl.ds(1, rows, stride=2), :]
        acc_ref[...] = jnp.zeros(acc_ref.shape, F32)

    @pl.when(ph == 1)
    def _():
        src_row = lax.broadcasted_iota(jnp.int32, (page, page * nh), 0)
        dst_row = lax.broadcasted_iota(jnp.int32, (page, page * nh), 1) >> (nh.bit_length() - 1)
        spread = jnp.where(src_row == dst_row, 1.0, 0.0).astype(BF16)
        head = lax.broadcasted_iota(jnp.int32, (nh, page * nh), 0)
        col_head = lax.broadcasted_iota(jnp.int32, (nh, page * nh), 1) & (nh - 1)
        acc = acc_ref[...]
        for gi in range(ppg):
            pg = j * ppg + gi
            d = d_ref[pl.ds(pl.multiple_of(pg * nh, nh), nh), :].astype(BF16)
            dexp = jnp.where(head == col_head, _dot(d, spread), 0.0).astype(BF16)
            acc = acc + _dot(dexp, v_refs[gi][...].reshape(page * nh, dv).astype(BF16))
        acc_ref[...] = acc

    @pl.when((ph == 1) & (j == nj - 1))
    def _():
        d_self = d_ref[n_past * nh:(n_past + 1) * nh, 0:1]
        o = acc_ref[...] + d_self * vn_ref[...]
        o_ref[...] = _sub_norm(o, g_ref[...], lam_init)


def _paged_attention(q, k_new, v_new, cache_k, cache_v, layer, page_table, lq, g, lam_init):
    bs, width = q.shape
    n_pages = page_table.shape[1]
    _, _, page, ncomp, dh = cache_k.shape
    nh, dv = cache_v.shape[3], cache_v.shape[4]
    ppg = math.gcd(PAGES_PER_STEP, n_pages)
    nj = n_pages // ppg
    cache_kt = jnp.transpose(cache_k, (0, 1, 3, 4, 2))

    def k_map(gi):
        return lambda b, ph, j, pt: (layer, pt[b * n_pages + jnp.where(ph == 0, j, nj - 1) * ppg + gi], 0, 0, 0)

    def v_map(gi):
        return lambda b, ph, j, pt: (layer, pt[b * n_pages + jnp.where(ph == 0, 0, j) * ppg + gi], 0, 0, 0)

    def row_spec(r, c):
        return pl.BlockSpec((None, r, c), lambda b, ph, j, pt: (b, 0, 0))

    in_specs = [pl.BlockSpec((4, D_HEAD), lambda b, ph, j, pt: (0, 0)),
                pl.BlockSpec((1, V_HEAD), lambda b, ph, j, pt: (0, 0)),
                row_spec(ncomp * dh, 1), row_spec(ncomp, dh), row_spec(ncomp, dh), row_spec(nh, dv)]
    in_specs += [pl.BlockSpec((None, None, ncomp, dh, page), k_map(gi)) for gi in range(ppg)]
    in_specs += [pl.BlockSpec((None, None, page, nh, dv), v_map(gi)) for gi in range(ppg)]
    qf = q.astype(F32)
    out = pl.pallas_call(
        functools.partial(_paged_kernel, ppg=ppg, nj=nj, lam_init=lam_init),
        grid_spec=pltpu.PrefetchScalarGridSpec(
            num_scalar_prefetch=1,
            grid=(bs, 2, nj),
            in_specs=in_specs,
            out_specs=row_spec(nh, dv),
            scratch_shapes=[pltpu.VMEM(((n_pages + 1) * ncomp, page), F32),
                            pltpu.VMEM(((n_pages + 1) * nh, page), F32),
                            pltpu.VMEM((nh, dv), F32)]),
        out_shape=jax.ShapeDtypeStruct((bs, nh, dv), F32),
        compiler_params=_params("arbitrary", "arbitrary", "arbitrary"),
        name="paged_attention",
    )(page_table.reshape(-1), lq, g, qf.reshape(bs, ncomp * dh, 1), qf.reshape(bs, ncomp, dh),
      k_new.reshape(bs, ncomp, dh), v_new.reshape(bs, nh, dv), *([cache_kt] * ppg), *([cache_v] * ppg))
    return out.reshape(bs, width)


HALO_ROWS = 32


def _conv_seq_kernel(u_ref, halo_ref, w_ref, b_ref, g_ref, beta_ref, cv_ref, ext_ref, y_ref, *, tpb):
    tm, d = u_ref.shape
    first = (pl.program_id(0) % tpb) == 0
    ext_ref[0:HALO_ROWS, :] = jnp.where(first, 0.0, halo_ref[...])
    ext_ref[HALO_ROWS:, :] = u_ref[...]
    lead = HALO_ROWS - (CONV_K - 1)
    rc = min(CONV_ROWS, tm)

    def chunk(r, carry):
        r0 = pl.multiple_of(r * rc, rc)
        for j in range(d // LANES):
            ls = slice(j * LANES, (j + 1) * LANES)
            blk = ext_ref[pl.ds(r0, rc + HALO_ROWS), ls]
            acc = jnp.zeros((rc, LANES), F32)
            for s in range(8):
                taps = [k for k in range(CONV_K) if (lead + k) % 8 == s]
                if not taps:
                    continue
                span = (lead + taps[-1]) - s + rc
                shifted = blk[s:s + span]
                for k in taps:
                    a = lead + k - s
                    acc = acc + w_ref[k:k + 1, ls] * shifted[a:a + rc]
            y_ref[pl.ds(r0, rc), ls] = acc + b_ref[:, ls]
        return carry

    lax.fori_loop(0, tm // rc, chunk, 0)
    cv_ref[...] = _silu(_layer_norm(y_ref[...], g_ref[...], beta_ref[...])).astype(BF16)


def _conv_seq(rows, u, w, b, g, beta):
    d = u.shape[1]
    tm = rows.tm
    const = lambda i: (0, 0)
    return pl.pallas_call(
        functools.partial(_conv_seq_kernel, tpb=rows.tpb),
        grid=(rows.nt,),
        in_specs=[pl.BlockSpec((tm, d), lambda i: (i, 0)),
                  pl.BlockSpec((HALO_ROWS, d), lambda i: (jnp.maximum(i * (tm // HALO_ROWS) - 1, 0), 0)),
                  pl.BlockSpec((CONV_K, d), const),
                  pl.BlockSpec((1, d), const), pl.BlockSpec((1, d), const), pl.BlockSpec((1, d), const)],
        out_specs=pl.BlockSpec((tm, d), lambda i: (i, 0)),
        out_shape=jax.ShapeDtypeStruct(u.shape, BF16),
        scratch_shapes=[pltpu.VMEM((tm + HALO_ROWS, d), F32), pltpu.VMEM((tm, d), F32)],
        compiler_params=_params("arbitrary"),
        name="conv_seq",
    )(u, u, w, b, g, beta)


def _conv_step_kernel(st_ref, u_ref, w_ref, b_ref, g_ref, beta_ref, cv_ref, acc_ref):
    k = pl.program_id(1)
    wk = w_ref[pl.ds(k, 1), :]

    @pl.when(k == 0)
    def _():
        acc_ref[...] = jnp.zeros(acc_ref.shape, F32)

    @pl.when(k < CONV_K - 1)
    def _():
        acc_ref[...] += st_ref[...] * wk

    @pl.when(k == CONV_K - 1)
    def _():
        y = acc_ref[...] + u_ref[...] * wk + b_ref[...]
        cv_ref[...] = _silu(_layer_norm(y, g_ref[...], beta_ref[...])).astype(BF16)


def _conv_step(rows, state, u, w, b, g, beta):
    d = u.shape[1]
    tm = rows.tm
    const = lambda i, k: (0, 0)
    row_spec = pl.BlockSpec((tm, d), lambda i, k: (i, 0))
    return pl.pallas_call(
        _conv_step_kernel,
        grid=(rows.nt, CONV_K),
        in_specs=[pl.BlockSpec((tm, d), lambda i, k: (i, jnp.minimum(k, CONV_K - 2))),
                  row_spec,
                  pl.BlockSpec((CONV_K, d), const),
                  pl.BlockSpec((1, d), const), pl.BlockSpec((1, d), const), pl.BlockSpec((1, d), const)],
        out_specs=row_spec,
        out_shape=jax.ShapeDtypeStruct(u.shape, BF16),
        scratch_shapes=[pltpu.VMEM((tm, d), F32)],
        compiler_params=_params("arbitrary", "arbitrary"),
        name="conv_step",
    )(state.reshape(state.shape[0], (CONV_K - 1) * d), u, w, b, g, beta)


def _mix_kernel(cv_ref, ga_ref, gb_ref, at_ref, x_ref, g1_ref, sc2_ref, sh2_ref, wp_ref, bp_ref, wo_ref,
                lg_ref, lb_ref, x1_ref, h2_ref, wpbf_ref, wobf_ref, *, alpha):
    @pl.when(pl.program_id(0) == 0)
    def _():
        wpbf_ref[...] = wp_ref[...].astype(BF16)
        wobf_ref[...] = wo_ref[...].astype(BF16)

    conv_out = _dot(cv_ref[...], wpbf_ref[...]) + bp_ref[...]
    merged = ga_ref[...].astype(F32) * conv_out + gb_ref[...].astype(F32) * at_ref[...].astype(F32)
    mix = _dot(merged.astype(BF16), wobf_ref[...])
    x1 = _layer_norm(alpha * x_ref[...] + g1_ref[...] * mix, lg_ref[...], lb_ref[...])
    x1_ref[...] = x1
    h2_ref[...] = (x1 * (1.0 + sc2_ref[...]) + sh2_ref[...]).astype(BF16)


def _mix(rows, cv, gates, attn, x, g1, sc2, sh2, w_pw2, b_pw2, w_o, ln_g, ln_b, alpha):
    d = x.shape[1]
    tm = rows.tm
    row_spec = pl.BlockSpec((tm, d), lambda i: (i, 0))
    const = lambda i: (0, 0)
    vec = pl.BlockSpec((1, d), const)
    mat = pl.BlockSpec((d, d), const)
    mod = rows.mod_spec(d, 1)
    return pl.pallas_call(
        functools.partial(_mix_kernel, alpha=alpha),
        grid=(rows.nt,),
        in_specs=[row_spec, row_spec, pl.BlockSpec((tm, d), lambda i: (i, 1)), row_spec, row_spec,
                  mod, mod, mod, mat, vec, mat, vec, vec],
        out_specs=[row_spec, row_spec],
        out_shape=[jax.ShapeDtypeStruct(x.shape, F32), jax.ShapeDtypeStruct(x.shape, BF16)],
        scratch_shapes=[pltpu.VMEM((d, d), BF16), pltpu.VMEM((d, d), BF16)],
        compiler_params=_params("arbitrary"),
        name="mix",
    )(cv, gates, gates, attn, x, g1, sc2, sh2, w_pw2, b_pw2, w_o, ln_g, ln_b)


FFN_HALO_ROWS = 8


def _ffn_in_kernel(h_ref, wg_ref, wu_ref, gt_ref, up_ref, *rest, tail):
    h = h_ref[...]
    gt = _dot(h, wg_ref[...])
    gt_ref[...] = gt.astype(gt_ref.dtype)
    up_ref[...] = _dot(h, wu_ref[...]).astype(BF16)
    if tail:
        rest[0][...] = gt[gt.shape[0] - tail:, :]


def _ffn_in(rows, h, w_ffn_in_bf, layer, ffn, gt_dtype, tail):
    d = h.shape[1]
    tm = rows.tm
    tn = ffn // 2 if (ffn // 2) % LANES == 0 else ffn
    nj = ffn // tn
    o_spec = pl.BlockSpec((tm, tn), lambda j, i: (i, j))
    out_specs = [o_spec, o_spec]
    out_shape = [jax.ShapeDtypeStruct((rows.m, ffn), gt_dtype), jax.ShapeDtypeStruct((rows.m, ffn), BF16)]
    if tail:
        out_specs.append(pl.BlockSpec((None, tail, tn), lambda j, i: (i // rows.tpb, 0, j)))
        out_shape.append(jax.ShapeDtypeStruct((rows.m // rows.seq, tail, ffn), F32))
    return pl.pallas_call(
        functools.partial(_ffn_in_kernel, tail=tail),
        grid=(nj, rows.nt),
        in_specs=[pl.BlockSpec((tm, d), lambda j, i: (i, 0)),
                  pl.BlockSpec((None, d, tn), lambda j, i: (layer, 0, j)),
                  pl.BlockSpec((None, d, tn), lambda j, i: (layer, 0, nj + j))],
        out_specs=out_specs,
        out_shape=out_shape,
        compiler_params=_params("arbitrary", "arbitrary"),
        name="ffn_in",
    )(h, w_ffn_in_bf, w_ffn_in_bf)


def _ffn_tail(act, x_ref, g2_ref, wd_ref, lg_ref, lb_ref, x2_ref, nxt, alpha):
    f = _dot(act.astype(BF16), wd_ref[...])
    x2 = _layer_norm(alpha * x_ref[...] + g2_ref[...] * f, lg_ref[...], lb_ref[...])
    x2_ref[...] = x2
    if nxt:
        sc_ref, sh_ref, hn_ref = nxt
        hn_ref[...] = (x2 * (1.0 + sc_ref[...]) + sh_ref[...]).astype(BF16)


def _ffn_down_seq_kernel(gt_ref, halo_ref, up_ref, x_ref, g2_ref, cw_ref, cb_ref, wd_ref, lg_ref, lb_ref,
                         *rest, tpb, alpha, has_next):
    if has_next:
        sc_ref, sh_ref, x2_ref, hn_ref, ext_ref = rest
        nxt = (sc_ref, sh_ref, hn_ref)
    else:
        x2_ref, ext_ref = rest
        nxt = None
    tm = gt_ref.shape[0]
    first = (pl.program_id(0) % tpb) == 0
    ext_ref[0:FFN_HALO_ROWS, :] = jnp.where(first, 0.0, halo_ref[...].astype(F32))
    ext_ref[FFN_HALO_ROWS:, :] = gt_ref[...].astype(F32)
    lead = FFN_HALO_ROWS - (FFN_CONV_K - 1)
    conv = cb_ref[...]
    for k in range(FFN_CONV_K):
        conv = conv + cw_ref[k:k + 1, :] * ext_ref[lead + k:lead + k + tm, :]
    _ffn_tail(_silu(conv) * up_ref[...].astype(F32), x_ref, g2_ref, wd_ref, lg_ref, lb_ref, x2_ref, nxt, alpha)


def _ffn_down_step_kernel(gt_ref, p2_ref, p1_ref, up_ref, x_ref, g2_ref, cw_ref, cb_ref, wd_ref, lg_ref, lb_ref,
                          *rest, alpha, has_next):
    if has_next:
        sc_ref, sh_ref, x2_ref, hn_ref = rest
        nxt = (sc_ref, sh_ref, hn_ref)
    else:
        (x2_ref,) = rest
        nxt = None
    conv = (cb_ref[...] + cw_ref[0:1, :] * p2_ref[...] + cw_ref[1:2, :] * p1_ref[...]
            + cw_ref[2:3, :] * gt_ref[...].astype(F32))
    _ffn_tail(_silu(conv) * up_ref[...].astype(F32), x_ref, g2_ref, wd_ref, lg_ref, lb_ref, x2_ref, nxt, alpha)


def _ffn_down(rows, gt, up, x, g2, conv_w, conv_b, w_down_bf, ln_g, ln_b, alpha, nxt, prev):
    m, d = x.shape
    ffn = gt.shape[1]
    tm = min(TM_FFN_DOWN, rows.tm)
    sub = rows.tm // tm
    tpb = rows.tpb * sub if rows.mod_rows == 1 else 1
    row_d = pl.BlockSpec((tm, d), lambda i: (i, 0))
    row_f = pl.BlockSpec((tm, ffn), lambda i: (i, 0))
    const = lambda i: (0, 0)
    vec_d = pl.BlockSpec((1, d), const)
    if rows.mod_rows == 1:
        mod = pl.BlockSpec((None, 1, d), lambda i: (i // tpb, 0, 0))
        mods = lambda a: a
    else:
        mod = pl.BlockSpec((tm, d), lambda i: (i, 0))
        mods = lambda a: a.reshape(m, d)
    common_in = [row_f, row_d, mod, pl.BlockSpec((FFN_CONV_K, ffn), const), pl.BlockSpec((1, ffn), const),
                 pl.BlockSpec((ffn, d), const), vec_d, vec_d]
    common_args = [up, x, mods(g2), conv_w, conv_b, w_down_bf, ln_g, ln_b]
    out_specs = [row_d]
    out_shape = [jax.ShapeDtypeStruct((m, d), F32)]
    if nxt:
        common_in += [mod, mod]
        common_args += [mods(nxt[0]), mods(nxt[1])]
        out_specs.append(row_d)
        out_shape.append(jax.ShapeDtypeStruct((m, d), BF16))
    if prev is None:
        body = functools.partial(_ffn_down_seq_kernel, tpb=tpb, alpha=alpha, has_next=bool(nxt))
        lead_in = [row_f, pl.BlockSpec((FFN_HALO_ROWS, ffn),
                                       lambda i: (jnp.maximum(i * (tm // FFN_HALO_ROWS) - 1, 0), 0))]
        lead_args = [gt, gt]
        scratch = [pltpu.VMEM((tm + FFN_HALO_ROWS, ffn), F32)]
    else:
        body = functools.partial(_ffn_down_step_kernel, alpha=alpha, has_next=bool(nxt))
        lead_in = [row_f, row_f, row_f]
        lead_args = [gt, prev[0], prev[1]]
        scratch = []
    res = pl.pallas_call(
        body,
        grid=(m // tm,),
        in_specs=lead_in + common_in,
        out_specs=out_specs,
        out_shape=out_shape,
        scratch_shapes=scratch,
        compiler_params=_params("arbitrary"),
        name="ffn_down",
    )(*lead_args, *common_args)
    return (res[0], res[1]) if nxt else (res[0], None)


def _rope_tables(pos):
    half = ROT_DIM // 2
    inv = ROPE_THETA ** (-jnp.arange(half, dtype=F32) / half)
    ang = pos.astype(F32)[:, None] * inv[None, :]
    cos, sin = jnp.cos(ang), jnp.sin(ang)
    t = pos.shape[0]
    zeros = jnp.zeros((t, half), F32)
    rest = D_HEAD - ROT_DIM
    c = jnp.concatenate([cos, cos, jnp.ones((t, rest), F32)], axis=1)
    sa = jnp.concatenate([-sin, zeros, jnp.zeros((t, rest), F32)], axis=1)
    sb = jnp.concatenate([zeros, sin, jnp.zeros((t, rest), F32)], axis=1)
    reps = LANES // D_HEAD
    return tuple(jnp.tile(a, (1, reps)) for a in (c, sa, sb))


def kernel(x_prompt, x_sample, cache_k, cache_v, state_conv, state_ffn_conv, page_table, c_prompt, c_sample,
           w_ada, b_ada, w_in, lambda_qk, attn_subln_g, glu_b, dw_w, dw_b, conv_ln_g, conv_ln_b, w_pw2, b_pw2,
           w_o, ln1_g, ln1_b, w_ffn_in, ffn_dw_w, ffn_dw_b, w_down, ln2_g, ln2_b):
    depth, d, _ = w_in.shape
    bp, seq, _ = x_prompt.shape
    bs, dec_seq, _ = x_sample.shape
    assert dec_seq == 1
    n_heads = cache_v.shape[3]
    qk_width = cache_k.shape[3] * cache_k.shape[4]
    v_width = n_heads * cache_v.shape[4]
    conv_ch = state_conv.shape[-1]
    ffn = state_ffn_conv.shape[-1]
    assert cache_k.shape[4] == D_HEAD and cache_v.shape[4] == V_HEAD and qk_width == v_width
    assert state_conv.shape[2] == CONV_K - 1 and state_ffn_conv.shape[2] == FFN_CONV_K - 1
    past = page_table.shape[1] * cache_k.shape[2]
    alpha = (2 * depth) ** 0.25
    q_scale = (D_HEAD ** -0.5) * LOG2E

    rows_p = _Rows(bp * seq, min(TM_PROMPT, seq), seq)
    rows_s = _Rows(bs, bs, 1)

    n_c = bp + bs
    pad = (-n_c) % 16
    c_all = jnp.concatenate([c_prompt, c_sample, jnp.zeros((pad, d), F32)], axis=0)
    ada = _ada(c_all, w_ada, b_ada)

    def mods(layer, rows, lo, hi):
        return [rows.mod(a) for a in jnp.split(ada[layer, lo:hi], 6, axis=-1)]

    mods_p = [mods(l, rows_p, 0, bp) for l in range(depth)]
    mods_s = [mods(l, rows_s, bp, n_c) for l in range(depth)]

    tables_p = _rope_tables(jnp.arange(seq))
    tables_s = _rope_tables(jnp.tile(past + jnp.arange(dec_seq), bs))

    w_ffn_in_bf = w_ffn_in.astype(BF16)
    w_down_bf = w_down.astype(BF16)
    row2 = lambda a: a.reshape(a.shape[0], 1, a.shape[1])

    def layer(l, rows, x, h, mod, nxt, tables, sample):
        sh1, sc1, g1, sh2, sc2, g2 = mod
        del sh1, sc1
        q = _proj_q(rows, h, w_in, l, tables, qk_width, q_scale)
        k, k_bf = _proj_kv(rows, h, w_in, l, qk_width, qk_width, tables)
        v, v_bf = _proj_kv(rows, h, w_in, l, 2 * qk_width, v_width, None)
        col = 2 * qk_width + v_width
        lam_init = 0.8 - 0.6 * math.exp(-0.3 * l)
        lq, sub_g = lambda_qk[l], attn_subln_g[l].reshape(1, V_HEAD)
        conv_args = (dw_w[l], row2(dw_b)[l], row2(conv_ln_g)[l], row2(conv_ln_b)[l])
        if sample:
            (u,) = _proj_glu(rows, h, w_in, glu_b, l, col, conv_ch, 0)
            attn = _paged_attention(q, k, v, cache_k, cache_v, l, page_table, lq, sub_g, lam_init)
            cv = _conv_step(rows, state_conv[l], u, *conv_args)
            conv_hist = jnp.concatenate([state_conv[l][:, 1:], u[:, None, :]], axis=1)
        else:
            u, u_tail = _proj_glu(rows, h, w_in, glu_b, l, col, conv_ch, HALO_ROWS)
            attn = _prompt_attention(q, k_bf, v_bf, lq, sub_g, bp, seq, lam_init)
            cv = _conv_seq(rows, u, *conv_args)
            conv_hist = u_tail[:, HALO_ROWS - (CONV_K - 1):]
        gates = _proj_gate(rows, h, w_in, l, col + 2 * conv_ch, 2 * d)
        x1, h2 = _mix(rows, cv, gates, attn, x, g1, sc2, sh2, w_pw2[l], row2(b_pw2)[l], w_o[l],
                      row2(ln1_g)[l], row2(ln1_b)[l], alpha)
        ffn_args = (ffn_dw_w[l], row2(ffn_dw_b)[l], w_down_bf[l], row2(ln2_g)[l], row2(ln2_b)[l], alpha, nxt)
        if sample:
            gt, up = _ffn_in(rows, h2, w_ffn_in_bf, l, ffn, F32, 0)
            hist = state_ffn_conv[l]
            x2, h_next = _ffn_down(rows, gt, up, x1, g2, *ffn_args, (hist[:, 0], hist[:, 1]))
            ffn_hist = jnp.concatenate([hist[:, 1:], gt[:, None, :]], axis=1)
        else:
            gt, up, gt_tail = _ffn_in(rows, h2, w_ffn_in_bf, l, ffn, BF16, FFN_HALO_ROWS)
            x2, h_next = _ffn_down(rows, gt, up, x1, g2, *ffn_args, None)
            ffn_hist = gt_tail[:, FFN_HALO_ROWS - (FFN_CONV_K - 1):]
        return x2, h_next, k, v, conv_hist, ffn_hist

    xp = x_prompt.reshape(bp * seq, d)
    xs = x_sample.reshape(bs, d)
    hp = _modulate(rows_p, xp, mods_p[0][1], mods_p[0][0])
    hs = _modulate(rows_s, xs, mods_s[0][1], mods_s[0][0])
    outs_p, outs_s = [], []
    for l in range(depth):
        nxt_p = (mods_p[l + 1][1], mods_p[l + 1][0]) if l + 1 < depth else None
        nxt_s = (mods_s[l + 1][1], mods_s[l + 1][0]) if l + 1 < depth else None
        xp, hp, *rest = layer(l, rows_p, xp, hp, mods_p[l], nxt_p, tables_p, False)
        outs_p.append(rest)
        xs, hs, *rest = layer(l, rows_s, xs, hs, mods_s[l], nxt_s, tables_s, True)
        outs_s.append(rest)

    def stack(outs, i, shape):
        return jnp.stack([o[i].reshape(shape) for o in outs])

    kd = cache_k.shape[3]
    return (xp.reshape(bp, seq, d), xs.reshape(bs, dec_seq, d),
            stack(outs_p, 0, (bp, seq, kd, D_HEAD)), stack(outs_p, 1, (bp, seq, n_heads, V_HEAD)),
            stack(outs_p, 2, (bp, CONV_K - 1, conv_ch)), stack(outs_p, 3, (bp, FFN_CONV_K - 1, ffn)),
            stack(outs_s, 0, (bs, dec_seq, kd, D_HEAD)), stack(outs_s, 1, (bs, dec_seq, n_heads, V_HEAD)),
            stack(outs_s, 2, (bs, CONV_K - 1, conv_ch)), stack(outs_s, 3, (bs, FFN_CONV_K - 1, ffn)))
```

```python
import functools
import math

import jax
import jax.numpy as jnp
from jax import lax
from jax.experimental import pallas as pl
from jax.experimental.pallas import tpu as pltpu

BF16 = jnp.bfloat16
F32 = jnp.float32

D_HEAD = 64
V_HEAD = 2 * D_HEAD
ROT_DIM = D_HEAD // 4
ROPE_THETA = 500000.0
CONV_K = 31
FFN_CONV_K = 3
LN_EPS = 1e-5
LANES = 128
NEG_BIG = -1e30
LOG2E = 1.4426950408889634
VMEM_LIMIT_BYTES = 56 * 1024 * 1024

TM_PROMPT = 1024
TN_PROJ = 1024
TM_FFN_DOWN = 512
TQ_ATTN = 256
TK_ATTN = 512
CONV_ROWS = 64
PAGES_PER_STEP = 16
HEADS_PER_STEP = 8
QK_AHEAD = 2


def _params(*sem):
    return pltpu.CompilerParams(dimension_semantics=sem, vmem_limit_bytes=VMEM_LIMIT_BYTES)


def _silu(z):
    return z * jax.nn.sigmoid(z)


def _layer_norm(y, g, b):
    mu = jnp.mean(y, axis=-1, keepdims=True)
    d = y - mu
    var = jnp.mean(d * d, axis=-1, keepdims=True)
    return d * lax.rsqrt(var + LN_EPS) * g + b


def _dot(a, b):
    return jnp.dot(a, b, preferred_element_type=F32)


def _ada_kernel(c_ref, w_ref, b_ref, o_ref):
    s = _silu(c_ref[...]).astype(BF16)
    o_ref[...] = _dot(s, w_ref[...].astype(BF16)) + b_ref[...]


def _ada(c_all, w_ada, b_ada):
    nl, d, n = w_ada.shape
    mc = c_all.shape[0]
    tn = 1536 if n % 1536 == 0 else n
    return pl.pallas_call(
        _ada_kernel,
        grid=(nl, n // tn),
        in_specs=[pl.BlockSpec((mc, d), lambda l, j: (0, 0)),
                  pl.BlockSpec((None, d, tn), lambda l, j: (l, 0, j)),
                  pl.BlockSpec((None, 1, tn), lambda l, j: (l, 0, j))],
        out_specs=pl.BlockSpec((None, mc, tn), lambda l, j: (l, 0, j)),
        out_shape=jax.ShapeDtypeStruct((nl, mc, n), F32),
        compiler_params=_params("arbitrary", "arbitrary"),
        name="ada",
    )(c_all, w_ada, b_ada.reshape(nl, 1, n))


class _Rows:
    def __init__(self, m, tm, seq):
        self.m, self.tm, self.seq = m, tm, seq
        self.nt = m // tm
        if seq >= tm:
            assert seq % tm == 0
            self.tpb, self.mod_rows = seq // tm, 1
        else:
            assert seq == 1
            self.tpb, self.mod_rows = 1, tm

    def mod(self, a):
        return a.reshape(-1, self.mod_rows, a.shape[-1])

    def mod_spec(self, d, grid_rank):
        if grid_rank == 1:
            return pl.BlockSpec((None, self.mod_rows, d), lambda i: (i // self.tpb, 0, 0))
        return pl.BlockSpec((None, self.mod_rows, d), lambda j, i: (i // self.tpb, 0, 0))


def _modulate_kernel(x_ref, sc_ref, sh_ref, h_ref):
    h_ref[...] = (x_ref[...] * (1.0 + sc_ref[...]) + sh_ref[...]).astype(BF16)


def _modulate(rows, x, sc, sh):
    d = x.shape[1]
    row_spec = pl.BlockSpec((rows.tm, d), lambda i: (i, 0))
    return pl.pallas_call(
        _modulate_kernel,
        grid=(rows.nt,),
        in_specs=[row_spec, rows.mod_spec(d, 1), rows.mod_spec(d, 1)],
        out_specs=row_spec,
        out_shape=jax.ShapeDtypeStruct(x.shape, BF16),
        compiler_params=_params("arbitrary"),
        name="modulate",
    )(x, sc, sh)


def _load_weight(w_ref, wbf_ref):
    @pl.when(pl.program_id(1) == 0)
    def _():
        wbf_ref[...] = w_ref[...].astype(BF16)


def _rope_chunk(y, c, sa, sb):
    return y * c + pltpu.roll(y, LANES - ROT_DIM // 2, 1) * sa + pltpu.roll(y, ROT_DIM // 2, 1) * sb


def _q_kernel(h_ref, w_ref, c_ref, sa_ref, sb_ref, q_ref, wbf_ref, *, scale):
    _load_weight(w_ref, wbf_ref)
    y = _dot(h_ref[...], wbf_ref[...])
    c, sa, sb = c_ref[...], sa_ref[...], sb_ref[...]
    for j in range(y.shape[1] // LANES):
        ls = slice(j * LANES, (j + 1) * LANES)
        q_ref[:, ls] = (_rope_chunk(y[:, ls], c, sa, sb) * scale).astype(BF16)


def _kv_kernel(h_ref, w_ref, *rest, rope):
    if rope:
        c_ref, sa_ref, sb_ref, o_ref, obf_ref, wbf_ref = rest
    else:
        o_ref, obf_ref, wbf_ref = rest
    _load_weight(w_ref, wbf_ref)
    y = _dot(h_ref[...], wbf_ref[...])
    if rope:
        c, sa, sb = c_ref[...], sa_ref[...], sb_ref[...]
        for j in range(y.shape[1] // LANES):
            ls = slice(j * LANES, (j + 1) * LANES)
            r = _rope_chunk(y[:, ls], c, sa, sb)
            o_ref[:, ls] = r
            obf_ref[:, ls] = r.astype(BF16)
    else:
        o_ref[...] = y
        obf_ref[...] = y.astype(BF16)


def _glu_kernel(h_ref, wa_ref, wb_ref, ba_ref, bb_ref, u_ref, *rest, tail):
    if tail:
        tail_ref, wabf_ref, wbbf_ref = rest
    else:
        wabf_ref, wbbf_ref = rest
    _load_weight(wa_ref, wabf_ref)
    _load_weight(wb_ref, wbbf_ref)
    h = h_ref[...]
    a = _dot(h, wabf_ref[...]) + ba_ref[...]
    b = _dot(h, wbbf_ref[...]) + bb_ref[...]
    u = a * jax.nn.sigmoid(b)
    u_ref[...] = u
    if tail:
        tail_ref[...] = u[u.shape[0] - tail:, :]


def _gate_kernel(h_ref, w_ref, o_ref, wbf_ref):
    _load_weight(w_ref, wbf_ref)
    o_ref[...] = jax.nn.sigmoid(_dot(h_ref[...], wbf_ref[...])).astype(BF16)


def _proj_specs(rows, d, tn, col0, layer):
    h_spec = pl.BlockSpec((rows.tm, d), lambda j, i: (i, 0))
    w_spec = pl.BlockSpec((None, d, tn), lambda j, i: (layer, 0, col0 // tn + j))
    o_spec = pl.BlockSpec((rows.tm, tn), lambda j, i: (i, j))
    return h_spec, w_spec, o_spec


def _table_specs(rows, tables):
    nt = tables[0].shape[0] // rows.tm
    return [pl.BlockSpec((rows.tm, LANES), lambda j, i: (i % nt, 0))] * 3


def _proj_q(rows, h, w_in, layer, tables, width, scale):
    d = h.shape[1]
    tn = min(TN_PROJ, width)
    h_spec, w_spec, o_spec = _proj_specs(rows, d, tn, 0, layer)
    return pl.pallas_call(
        functools.partial(_q_kernel, scale=scale),
        grid=(width // tn, rows.nt),
        in_specs=[h_spec, w_spec] + _table_specs(rows, tables),
        out_specs=o_spec,
        out_shape=jax.ShapeDtypeStruct((rows.m, width), BF16),
        scratch_shapes=[pltpu.VMEM((d, tn), BF16)],
        compiler_params=_params("arbitrary", "arbitrary"),
        name="proj_q",
    )(h, w_in, *tables)


def _proj_kv(rows, h, w_in, layer, col0, width, tables):
    d = h.shape[1]
    tn = min(TN_PROJ, width)
    h_spec, w_spec, o_spec = _proj_specs(rows, d, tn, col0, layer)
    rope = tables is not None
    extra = _table_specs(rows, tables) if rope else []
    args = tuple(tables) if rope else ()
    return pl.pallas_call(
        functools.partial(_kv_kernel, rope=rope),
        grid=(width // tn, rows.nt),
        in_specs=[h_spec, w_spec] + extra,
        out_specs=[o_spec, o_spec],
        out_shape=[jax.ShapeDtypeStruct((rows.m, width), F32),
                   jax.ShapeDtypeStruct((rows.m, width), BF16)],
        scratch_shapes=[pltpu.VMEM((d, tn), BF16)],
        compiler_params=_params("arbitrary", "arbitrary"),
        name="proj_k" if rope else "proj_v",
    )(h, w_in, *args)


def _proj_glu(rows, h, w_in, glu_b, layer, col0, width, tail):
    d = h.shape[1]
    tn = min(TN_PROJ, width)
    nl = glu_b.shape[0]
    h_spec, wa_spec, o_spec = _proj_specs(rows, d, tn, col0, layer)
    _, wb_spec, _ = _proj_specs(rows, d, tn, col0 + width, layer)
    ba_spec = pl.BlockSpec((None, 1, tn), lambda j, i: (layer, 0, j))
    bb_spec = pl.BlockSpec((None, 1, tn), lambda j, i: (layer, 0, width // tn + j))
    out_specs = [o_spec]
    out_shape = [jax.ShapeDtypeStruct((rows.m, width), F32)]
    if tail:
        out_specs.append(pl.BlockSpec((None, tail, tn), lambda j, i: (i // rows.tpb, 0, j)))
        out_shape.append(jax.ShapeDtypeStruct((rows.m // rows.seq, tail, width), F32))
    gb = glu_b.reshape(nl, 1, 2 * width)
    return pl.pallas_call(
        functools.partial(_glu_kernel, tail=tail),
        grid=(width // tn, rows.nt),
        in_specs=[h_spec, wa_spec, wb_spec, ba_spec, bb_spec],
        out_specs=out_specs,
        out_shape=out_shape,
        scratch_shapes=[pltpu.VMEM((d, tn), BF16), pltpu.VMEM((d, tn), BF16)],
        compiler_params=_params("arbitrary", "arbitrary"),
        name="proj_glu",
    )(h, w_in, w_in, gb, gb)


def _proj_gate(rows, h, w_in, layer, col0, width):
    d = h.shape[1]
    tn = min(TN_PROJ, width)
    h_spec, w_spec, o_spec = _proj_specs(rows, d, tn, col0, layer)
    return pl.pallas_call(
        _gate_kernel,
        grid=(width // tn, rows.nt),
        in_specs=[h_spec, w_spec],
        out_specs=o_spec,
        out_shape=jax.ShapeDtypeStruct((rows.m, width), BF16),
        scratch_shapes=[pltpu.VMEM((d, tn), BF16)],
        compiler_params=_params("arbitrary", "arbitrary"),
        name="proj_gate",
    )(h, w_in)


def _lambda(lq_ref, lam_init):
    lq = lq_ref[...]
    a = jnp.sum(lq[0:1] * lq[1:2], axis=1, keepdims=True)
    b = jnp.sum(lq[2:3] * lq[3:4], axis=1, keepdims=True)
    return jnp.exp(a) - jnp.exp(b) + lam_init


def _sub_norm(o, g, lam_init):
    ms = jnp.mean(o * o, axis=-1, keepdims=True)
    return o * lax.rsqrt(ms + LN_EPS) * g * (1.0 - lam_init)


def _attn_kernel(lq_ref, g_ref, q_ref, k_ref, vt_ref, o_ref, acc_ref, m_ref, l_ref, *, tq, tk, hps, lam_init):
    i = pl.program_id(2)
    q2 = []
    for hh in range(hps):
        q = q_ref[:, hh * V_HEAD:(hh + 1) * V_HEAD]
        lane = lax.broadcasted_iota(jnp.int32, q.shape, 1)
        zero = jnp.zeros_like(q)
        q2.append(jnp.concatenate([jnp.where(lane < D_HEAD, q, zero), jnp.where(lane >= D_HEAD, q, zero)],
                                  axis=0))
    m_ref[...] = jnp.full(m_ref.shape, NEG_BIG, F32)
    l_ref[...] = jnp.zeros(l_ref.shape, F32)
    acc_ref[...] = jnp.zeros(acc_ref.shape, F32)

    def step(j, masked, part=None):
        pi, pn = part if part else (0, 1)
        nk = tk // pn
        k0 = pl.multiple_of(j * tk + pi * nk, nk)

        def qk(hh):
            kt = k_ref[pl.ds(k0, nk), hh * V_HEAD:(hh + 1) * V_HEAD]
            return lax.dot_general(kt, q2[hh], (((1,), (1,)), ((), ())), preferred_element_type=F32)

        scores = [qk(hh) for hh in range(min(QK_AHEAD, hps))]
        for hh in range(hps):
            if hh + QK_AHEAD < hps:
                scores.append(qk(hh + QK_AHEAD))
            s = scores[hh]
            if masked:
                kpos = k0 + lax.broadcasted_iota(jnp.int32, s.shape, 0)
                qpos = i * tq + (lax.broadcasted_iota(jnp.int32, s.shape, 1) & (tq - 1))
                s = jnp.where(kpos <= qpos, s, NEG_BIG)
            m_prev = m_ref[hh]
            m_new = jnp.maximum(m_prev, jnp.max(s, axis=0, keepdims=True))
            alpha = jnp.exp2(m_prev - m_new)
            p = jnp.exp2(s - m_new)
            l_ref[hh] = alpha * l_ref[hh] + jnp.sum(p, axis=0, keepdims=True)
            vt = vt_ref[hh, j, :, pi * nk:(pi + 1) * nk]
            acc_ref[hh] = acc_ref[hh] * alpha + _dot(vt, p.astype(BF16))
            m_ref[hh] = m_new

    n_full = (i * tq) // tk

    def full_step(j, carry):
        step(j, False)
        return carry

    lax.fori_loop(0, n_full, full_step, 0)
    parts = tk // tq
    diag = i % parts
    for pi in range(parts):
        if pi < parts - 1:
            @pl.when(pi < diag)
            def _(pi=pi):
                step(n_full, False, (pi, parts))

        @pl.when(pi == diag)
        def _(pi=pi):
            step(n_full, True, (pi, parts))

    lam = _lambda(lq_ref, lam_init)
    for hh in range(hps):
        o2 = acc_ref[hh] * (1.0 / l_ref[hh])
        ot = o2[:, :tq] - lam * o2[:, tq:]
        o_ref[:, hh * V_HEAD:(hh + 1) * V_HEAD] = _sub_norm(ot.T, g_ref[...], lam_init).astype(BF16)


def _prompt_attention(q, k, v, lq, g, batch, seq, lam_init):
    width = q.shape[1]
    nh = width // V_HEAD
    tq = min(TQ_ATTN, seq)
    tk = min(TK_ATTN, seq)
    hps = math.gcd(HEADS_PER_STEP, nh)
    assert tq & (tq - 1) == 0 and tk % tq == 0 and seq % tk == 0
    q3 = q.reshape(batch, seq, width)
    k3 = k.reshape(batch, seq, width)
    vt = v.reshape(batch, seq // tk, tk, nh, V_HEAD).transpose(0, 3, 1, 4, 2)
    gw = hps * V_HEAD
    out = pl.pallas_call(
        functools.partial(_attn_kernel, tq=tq, tk=tk, hps=hps, lam_init=lam_init),
        grid=(batch, nh // hps, seq // tq),
        in_specs=[pl.BlockSpec((4, D_HEAD), lambda b, h, i: (0, 0)),
                  pl.BlockSpec((1, V_HEAD), lambda b, h, i: (0, 0)),
                  pl.BlockSpec((None, tq, gw), lambda b, h, i: (b, i, h)),
                  pl.BlockSpec((None, seq, gw), lambda b, h, i: (b, 0, h)),
                  pl.BlockSpec((None, hps, seq // tk, V_HEAD, tk), lambda b, h, i: (b, h, 0, 0, 0))],
        out_specs=pl.BlockSpec((None, tq, gw), lambda b, h, i: (b, i, h)),
        out_shape=jax.ShapeDtypeStruct((batch, seq, width), BF16),
        scratch_shapes=[pltpu.VMEM((hps, V_HEAD, 2 * tq), F32),
                        pltpu.VMEM((hps, 1, 2 * tq), F32),
                        pltpu.VMEM((hps, 1, 2 * tq), F32)],
        compiler_params=_params("arbitrary", "arbitrary", "arbitrary"),
        name="prompt_attention",
    )(lq, g, q3, k3, vt)
    return out.reshape(batch * seq, width)


def _paged_kernel(pt_ref, lq_ref, g_ref, qcol_ref, q_ref, kn_ref, vn_ref, *rest, ppg, nj, lam_init):
    del pt_ref
    k_refs, v_refs = rest[:ppg], rest[ppg:2 * ppg]
    o_ref, s_ref, d_ref, acc_ref = rest[2 * ppg:]
    ph, j = pl.program_id(1), pl.program_id(2)
    ncomp, dh, page = k_refs[0].shape
    nh, dv = v_refs[0].shape[1], v_refs[0].shape[2]
    n_past = nj * ppg
    assert nh & (nh - 1) == 0

    @pl.when(ph == 0)
    def _():
        qcol = jnp.broadcast_to(qcol_ref[...], (ncomp * dh, page)).reshape(ncomp, dh, page)
        for gi in range(ppg):
            pg = j * ppg + gi
            s_ref[pl.ds(pl.multiple_of(pg * ncomp, ncomp), ncomp), :] = jnp.sum(k_refs[gi][...] * qcol, axis=1)

    @pl.when((ph == 0) & (j == nj - 1))
    def _():
        s_self = jnp.sum(q_ref[...] * kn_ref[...], axis=1, keepdims=True)
        lane = lax.broadcasted_iota(jnp.int32, (ncomp, page), 1)
        s_ref[n_past * ncomp:(n_past + 1) * ncomp, :] = jnp.where(lane == 0, s_self, NEG_BIG)
        s = s_ref[...].reshape(n_past + 1, ncomp, page)
        m = jnp.max(jnp.max(s, axis=0), axis=1, keepdims=True)
        p = jnp.exp2(s - m[None])
        inv = 1.0 / jnp.sum(jnp.sum(p, axis=0), axis=1, keepdims=True)
        s_ref[...] = (p * inv[None]).reshape((n_past + 1) * ncomp, page)
        lam = _lambda(lq_ref, lam_init)
        rows = (n_past + 1) * nh
        d_ref[...] = s_ref[pl.ds(0, rows, stride=2), :] - lam * s_ref[pl.ds(1, rows, stride=2), :]
        acc_ref[...] = jnp.zeros(acc_ref.shape, F32)

    @pl.when(ph == 1)
    def _():
        src_row = lax.broadcasted_iota(jnp.int32, (page, page * nh), 0)
        dst_row = lax.broadcasted_iota(jnp.int32, (page, page * nh), 1) >> (nh.bit_length() - 1)
        spread = jnp.where(src_row == dst_row, 1.0, 0.0).astype(BF16)
        head = lax.broadcasted_iota(jnp.int32, (nh, page * nh), 0)
        col_head = lax.broadcasted_iota(jnp.int32, (nh, page * nh), 1) & (nh - 1)
        acc = acc_ref[...]
        for gi in range(ppg):
            pg = j * ppg + gi
            d = d_ref[pl.ds(pl.multiple_of(pg * nh, nh), nh), :].astype(BF16)
            dexp = jnp.where(head == col_head, _dot(d, spread), 0.0).astype(BF16)
            acc = acc + _dot(dexp, v_refs[gi][...].reshape(page * nh, dv).astype(BF16))
        acc_ref[...] = acc

    @pl.when((ph == 1) & (j == nj - 1))
    def _():
        d_self = d_ref[n_past * nh:(n_past + 1) * nh, 0:1]
        o = acc_ref[...] + d_self * vn_ref[...]
        o_ref[...] = _sub_norm(o, g_ref[...], lam_init)


def _paged_attention(q, k_new, v_new, cache_k, cache_v, layer, page_table, lq, g, lam_init):
    bs, width = q.shape
    n_pages = page_table.shape[1]
    _, _, page, ncomp, dh = cache_k.shape
    nh, dv = cache_v.shape[3], cache_v.shape[4]
    ppg = math.gcd(PAGES_PER_STEP, n_pages)
    nj = n_pages // ppg
    cache_kt = jnp.transpose(cache_k, (0, 1, 3, 4, 2))

    def k_map(gi):
        return lambda b, ph, j, pt: (layer, pt[b * n_pages + jnp.where(ph == 0, j, nj - 1) * ppg + gi], 0, 0, 0)

    def v_map(gi):
        return lambda b, ph, j, pt: (layer, pt[b * n_pages + jnp.where(ph == 0, 0, j) * ppg + gi], 0, 0, 0)

    def row_spec(r, c):
        return pl.BlockSpec((None, r, c), lambda b, ph, j, pt: (b, 0, 0))

    in_specs = [pl.BlockSpec((4, D_HEAD), lambda b, ph, j, pt: (0, 0)),
                pl.BlockSpec((1, V_HEAD), lambda b, ph, j, pt: (0, 0)),
                row_spec(ncomp * dh, 1), row_spec(ncomp, dh), row_spec(ncomp, dh), row_spec(nh, dv)]
    in_specs += [pl.BlockSpec((None, None, ncomp, dh, page), k_map(gi)) for gi in range(ppg)]
    in_specs += [pl.BlockSpec((None, None, page, nh, dv), v_map(gi)) for gi in range(ppg)]
    qf = q.astype(F32)
    out = pl.pallas_call(
        functools.partial(_paged_kernel, ppg=ppg, nj=nj, lam_init=lam_init),
        grid_spec=pltpu.PrefetchScalarGridSpec(
            num_scalar_prefetch=1,
            grid=(bs, 2, nj),
            in_specs=in_specs,
            out_specs=row_spec(nh, dv),
            scratch_shapes=[pltpu.VMEM(((n_pages + 1) * ncomp, page), F32),
                            pltpu.VMEM(((n_pages + 1) * nh, page), F32),
                            pltpu.VMEM((nh, dv), F32)]),
        out_shape=jax.ShapeDtypeStruct((bs, nh, dv), F32),
        compiler_params=_params("arbitrary", "arbitrary", "arbitrary"),
        name="paged_attention",
    )(page_table.reshape(-1), lq, g, qf.reshape(bs, ncomp * dh, 1), qf.reshape(bs, ncomp, dh),
      k_new.reshape(bs, ncomp, dh), v_new.reshape(bs, nh, dv), *([cache_kt] * ppg), *([cache_v] * ppg))
    return out.reshape(bs, width)


HALO_ROWS = 32


def _conv_seq_kernel(u_ref, halo_ref, w_ref, b_ref, g_ref, beta_ref, cv_ref, ext_ref, y_ref, wb_ref, *, tpb):
    tm, d = u_ref.shape
    first = (pl.program_id(0) % tpb) == 0
    ext_ref[0:HALO_ROWS, :] = jnp.where(first, 0.0, halo_ref[...])
    ext_ref[HALO_ROWS:, :] = u_ref[...]
    for k in range(CONV_K):
        wb_ref[k] = jnp.broadcast_to(w_ref[k:k + 1, :], (8, d))
    lead = HALO_ROWS - (CONV_K - 1)
    rc = min(CONV_ROWS, tm)

    def chunk(r, carry):
        r0 = pl.multiple_of(r * rc, rc)
        for j in range(d // LANES):
            ls = slice(j * LANES, (j + 1) * LANES)
            blk = ext_ref[pl.ds(r0, rc + HALO_ROWS), ls]
            acc = jnp.zeros((rc // 8, 8, LANES), F32)
            for s in range(8):
                taps = [k for k in range(CONV_K) if (lead + k) % 8 == s]
                if not taps:
                    continue
                shifted = pltpu.roll(blk, rc + HALO_ROWS - s, 0) if s else blk
                for k in taps:
                    a = lead + k - s
                    acc = acc + wb_ref[k, :, ls][None] * shifted[a:a + rc].reshape(rc // 8, 8, LANES)
            y_ref[pl.ds(r0, rc), ls] = acc.reshape(rc, LANES) + b_ref[:, ls]
        return carry

    lax.fori_loop(0, tm // rc, chunk, 0)
    cv_ref[...] = _silu(_layer_norm(y_ref[...], g_ref[...], beta_ref[...])).astype(BF16)


def _conv_seq(rows, u, w, b, g, beta):
    d = u.shape[1]
    tm = rows.tm
    const = lambda i: (0, 0)
    return pl.pallas_call(
        functools.partial(_conv_seq_kernel, tpb=rows.tpb),
        grid=(rows.nt,),
        in_specs=[pl.BlockSpec((tm, d), lambda i: (i, 0)),
                  pl.BlockSpec((HALO_ROWS, d), lambda i: (jnp.maximum(i * (tm // HALO_ROWS) - 1, 0), 0)),
                  pl.BlockSpec((CONV_K, d), const),
                  pl.BlockSpec((1, d), const), pl.BlockSpec((1, d), const), pl.BlockSpec((1, d), const)],
        out_specs=pl.BlockSpec((tm, d), lambda i: (i, 0)),
        out_shape=jax.ShapeDtypeStruct(u.shape, BF16),
        scratch_shapes=[pltpu.VMEM((tm + HALO_ROWS, d), F32), pltpu.VMEM((tm, d), F32),
                        pltpu.VMEM((CONV_K, 8, d), F32)],
        compiler_params=_params("arbitrary"),
        name="conv_seq",
    )(u, u, w, b, g, beta)


def _conv_step_kernel(st_ref, u_ref, w_ref, b_ref, g_ref, beta_ref, cv_ref, acc_ref):
    k = pl.program_id(1)
    wk = w_ref[pl.ds(k, 1), :]

    @pl.when(k == 0)
    def _():
        acc_ref[...] = jnp.zeros(acc_ref.shape, F32)

    @pl.when(k < CONV_K - 1)
    def _():
        acc_ref[...] += st_ref[...] * wk

    @pl.when(k == CONV_K - 1)
    def _():
        y = acc_ref[...] + u_ref[...] * wk + b_ref[...]
        cv_ref[...] = _silu(_layer_norm(y, g_ref[...], beta_ref[...])).astype(BF16)


def _conv_step(rows, state, u, w, b, g, beta):
    d = u.shape[1]
    tm = rows.tm
    const = lambda i, k: (0, 0)
    row_spec = pl.BlockSpec((tm, d), lambda i, k: (i, 0))
    return pl.pallas_call(
        _conv_step_kernel,
        grid=(rows.nt, CONV_K),
        in_specs=[pl.BlockSpec((tm, d), lambda i, k: (i, jnp.minimum(k, CONV_K - 2))),
                  row_spec,
                  pl.BlockSpec((CONV_K, d), const),
                  pl.BlockSpec((1, d), const), pl.BlockSpec((1, d), const), pl.BlockSpec((1, d), const)],
        out_specs=row_spec,
        out_shape=jax.ShapeDtypeStruct(u.shape, BF16),
        scratch_shapes=[pltpu.VMEM((tm, d), F32)],
        compiler_params=_params("arbitrary", "arbitrary"),
        name="conv_step",
    )(state.reshape(state.shape[0], (CONV_K - 1) * d), u, w, b, g, beta)


def _mix_kernel(cv_ref, ga_ref, gb_ref, at_ref, x_ref, g1_ref, sc2_ref, sh2_ref, wp_ref, bp_ref, wo_ref,
                lg_ref, lb_ref, x1_ref, h2_ref, wpbf_ref, wobf_ref, *, alpha):
    @pl.when(pl.program_id(0) == 0)
    def _():
        wpbf_ref[...] = wp_ref[...].astype(BF16)
        wobf_ref[...] = wo_ref[...].astype(BF16)

    conv_out = _dot(cv_ref[...], wpbf_ref[...]) + bp_ref[...]
    merged = ga_ref[...].astype(F32) * conv_out + gb_ref[...].astype(F32) * at_ref[...].astype(F32)
    mix = _dot(merged.astype(BF16), wobf_ref[...])
    x1 = _layer_norm(alpha * x_ref[...] + g1_ref[...] * mix, lg_ref[...], lb_ref[...])
    x1_ref[...] = x1
    h2_ref[...] = (x1 * (1.0 + sc2_ref[...]) + sh2_ref[...]).astype(BF16)


def _mix(rows, cv, gates, attn, x, g1, sc2, sh2, w_pw2, b_pw2, w_o, ln_g, ln_b, alpha):
    d = x.shape[1]
    tm = rows.tm
    row_spec = pl.BlockSpec((tm, d), lambda i: (i, 0))
    const = lambda i: (0, 0)
    vec = pl.BlockSpec((1, d), const)
    mat = pl.BlockSpec((d, d), const)
    mod = rows.mod_spec(d, 1)
    return pl.pallas_call(
        functools.partial(_mix_kernel, alpha=alpha),
        grid=(rows.nt,),
        in_specs=[row_spec, row_spec, pl.BlockSpec((tm, d), lambda i: (i, 1)), row_spec, row_spec,
                  mod, mod, mod, mat, vec, mat, vec, vec],
        out_specs=[row_spec, row_spec],
        out_shape=[jax.ShapeDtypeStruct(x.shape, F32), jax.ShapeDtypeStruct(x.shape, BF16)],
        scratch_shapes=[pltpu.VMEM((d, d), BF16), pltpu.VMEM((d, d), BF16)],
        compiler_params=_params("arbitrary"),
        name="mix",
    )(cv, gates, gates, attn, x, g1, sc2, sh2, w_pw2, b_pw2, w_o, ln_g, ln_b)


FFN_HALO_ROWS = 8


def _ffn_in_kernel(h_ref, wg_ref, wu_ref, gt_ref, up_ref, *rest, tail):
    h = h_ref[...]
    gt = _dot(h, wg_ref[...])
    gt_ref[...] = gt.astype(gt_ref.dtype)
    up_ref[...] = _dot(h, wu_ref[...]).astype(BF16)
    if tail:
        rest[0][...] = gt[gt.shape[0] - tail:, :]


def _ffn_in(rows, h, w_ffn_in_bf, layer, ffn, gt_dtype, tail):
    d = h.shape[1]
    tm = rows.tm
    tn = ffn // 2 if (ffn // 2) % LANES == 0 else ffn
    nj = ffn // tn
    o_spec = pl.BlockSpec((tm, tn), lambda j, i: (i, j))
    out_specs = [o_spec, o_spec]
    out_shape = [jax.ShapeDtypeStruct((rows.m, ffn), gt_dtype), jax.ShapeDtypeStruct((rows.m, ffn), BF16)]
    if tail:
        out_specs.append(pl.BlockSpec((None, tail, tn), lambda j, i: (i // rows.tpb, 0, j)))
        out_shape.append(jax.ShapeDtypeStruct((rows.m // rows.seq, tail, ffn), F32))
    return pl.pallas_call(
        functools.partial(_ffn_in_kernel, tail=tail),
        grid=(nj, rows.nt),
        in_specs=[pl.BlockSpec((tm, d), lambda j, i: (i, 0)),
                  pl.BlockSpec((None, d, tn), lambda j, i: (layer, 0, j)),
                  pl.BlockSpec((None, d, tn), lambda j, i: (layer, 0, nj + j))],
        out_specs=out_specs,
        out_shape=out_shape,
        compiler_params=_params("arbitrary", "arbitrary"),
        name="ffn_in",
    )(h, w_ffn_in_bf, w_ffn_in_bf)


def _ffn_tail(act, x_ref, g2_ref, wd_ref, lg_ref, lb_ref, x2_ref, nxt, alpha):
    f = _dot(act.astype(BF16), wd_ref[...])
    x2 = _layer_norm(alpha * x_ref[...] + g2_ref[...] * f, lg_ref[...], lb_ref[...])
    x2_ref[...] = x2
    if nxt:
        sc_ref, sh_ref, hn_ref = nxt
        hn_ref[...] = (x2 * (1.0 + sc_ref[...]) + sh_ref[...]).astype(BF16)


def _ffn_down_seq_kernel(gt_ref, halo_ref, up_ref, x_ref, g2_ref, cw_ref, cb_ref, wd_ref, lg_ref, lb_ref,
                         *rest, tpb, alpha, has_next):
    if has_next:
        sc_ref, sh_ref, x2_ref, hn_ref, ext_ref = rest
        nxt = (sc_ref, sh_ref, hn_ref)
    else:
        x2_ref, ext_ref = rest
        nxt = None
    tm = gt_ref.shape[0]
    first = (pl.program_id(0) % tpb) == 0
    ext_ref[0:FFN_HALO_ROWS, :] = jnp.where(first, 0.0, halo_ref[...].astype(F32))
    ext_ref[FFN_HALO_ROWS:, :] = gt_ref[...].astype(F32)
    lead = FFN_HALO_ROWS - (FFN_CONV_K - 1)
    ext = ext_ref[...]
    conv = cb_ref[...]
    for k in range(FFN_CONV_K):
        off = lead + k
        shifted = pltpu.roll(ext, tm + FFN_HALO_ROWS - off, 0)[0:tm] if off % 8 else ext[off:off + tm]
        conv = conv + cw_ref[k:k + 1, :] * shifted
    _ffn_tail(_silu(conv) * up_ref[...].astype(F32), x_ref, g2_ref, wd_ref, lg_ref, lb_ref, x2_ref, nxt, alpha)


def _ffn_down_step_kernel(gt_ref, p2_ref, p1_ref, up_ref, x_ref, g2_ref, cw_ref, cb_ref, wd_ref, lg_ref, lb_ref,
                          *rest, alpha, has_next):
    if has_next:
        sc_ref, sh_ref, x2_ref, hn_ref = rest
        nxt = (sc_ref, sh_ref, hn_ref)
    else:
        (x2_ref,) = rest
        nxt = None
    conv = (cb_ref[...] + cw_ref[0:1, :] * p2_ref[...] + cw_ref[1:2, :] * p1_ref[...]
            + cw_ref[2:3, :] * gt_ref[...].astype(F32))
    _ffn_tail(_silu(conv) * up_ref[...].astype(F32), x_ref, g2_ref, wd_ref, lg_ref, lb_ref, x2_ref, nxt, alpha)


def _ffn_down(rows, gt, up, x, g2, conv_w, conv_b, w_down_bf, ln_g, ln_b, alpha, nxt, prev):
    m, d = x.shape
    ffn = gt.shape[1]
    tm = min(TM_FFN_DOWN, rows.tm)
    sub = rows.tm // tm
    tpb = rows.tpb * sub if rows.mod_rows == 1 else 1
    row_d = pl.BlockSpec((tm, d), lambda i: (i, 0))
    row_f = pl.BlockSpec((tm, ffn), lambda i: (i, 0))
    const = lambda i: (0, 0)
    vec_d = pl.BlockSpec((1, d), const)
    if rows.mod_rows == 1:
        mod = pl.BlockSpec((None, 1, d), lambda i: (i // tpb, 0, 0))
        mods = lambda a: a
    else:
        mod = pl.BlockSpec((tm, d), lambda i: (i, 0))
        mods = lambda a: a.reshape(m, d)
    common_in = [row_f, row_d, mod, pl.BlockSpec((FFN_CONV_K, ffn), const), pl.BlockSpec((1, ffn), const),
                 pl.BlockSpec((ffn, d), const), vec_d, vec_d]
    common_args = [up, x, mods(g2), conv_w, conv_b, w_down_bf, ln_g, ln_b]
    out_specs = [row_d]
    out_shape = [jax.ShapeDtypeStruct((m, d), F32)]
    if nxt:
        common_in += [mod, mod]
        common_args += [mods(nxt[0]), mods(nxt[1])]
        out_specs.append(row_d)
        out_shape.append(jax.ShapeDtypeStruct((m, d), BF16))
    if prev is None:
        body = functools.partial(_ffn_down_seq_kernel, tpb=tpb, alpha=alpha, has_next=bool(nxt))
        lead_in = [row_f, pl.BlockSpec((FFN_HALO_ROWS, ffn),
                                       lambda i: (jnp.maximum(i * (tm // FFN_HALO_ROWS) - 1, 0), 0))]
        lead_args = [gt, gt]
        scratch = [pltpu.VMEM((tm + FFN_HALO_ROWS, ffn), F32)]
    else:
        body = functools.partial(_ffn_down_step_kernel, alpha=alpha, has_next=bool(nxt))
        lead_in = [row_f, row_f, row_f]
        lead_args = [gt, prev[0], prev[1]]
        scratch = []
    res = pl.pallas_call(
        body,
        grid=(m // tm,),
        in_specs=lead_in + common_in,
        out_specs=out_specs,
        out_shape=out_shape,
        scratch_shapes=scratch,
        compiler_params=_params("arbitrary"),
        name="ffn_down",
    )(*lead_args, *common_args)
    return (res[0], res[1]) if nxt else (res[0], None)


def _rope_tables(pos):
    half = ROT_DIM // 2
    inv = ROPE_THETA ** (-jnp.arange(half, dtype=F32) / half)
    ang = pos.astype(F32)[:, None] * inv[None, :]
    cos, sin = jnp.cos(ang), jnp.sin(ang)
    t = pos.shape[0]
    zeros = jnp.zeros((t, half), F32)
    rest = D_HEAD - ROT_DIM
    c = jnp.concatenate([cos, cos, jnp.ones((t, rest), F32)], axis=1)
    sa = jnp.concatenate([-sin, zeros, jnp.zeros((t, rest), F32)], axis=1)
    sb = jnp.concatenate([zeros, sin, jnp.zeros((t, rest), F32)], axis=1)
    reps = LANES // D_HEAD
    return tuple(jnp.tile(a, (1, reps)) for a in (c, sa, sb))


def kernel(x_prompt, x_sample, cache_k, cache_v, state_conv, state_ffn_conv, page_table, c_prompt, c_sample,
           w_ada, b_ada, w_in, lambda_qk, attn_subln_g, glu_b, dw_w, dw_b, conv_ln_g, conv_ln_b, w_pw2, b_pw2,
           w_o, ln1_g, ln1_b, w_ffn_in, ffn_dw_w, ffn_dw_b, w_down, ln2_g, ln2_b):
    depth, d, _ = w_in.shape
    bp, seq, _ = x_prompt.shape
    bs, dec_seq, _ = x_sample.shape
    assert dec_seq == 1
    n_heads = cache_v.shape[3]
    qk_width = cache_k.shape[3] * cache_k.shape[4]
    v_width = n_heads * cache_v.shape[4]
    conv_ch = state_conv.shape[-1]
    ffn = state_ffn_conv.shape[-1]
    assert cache_k.shape[4] == D_HEAD and cache_v.shape[4] == V_HEAD and qk_width == v_width
    assert state_conv.shape[2] == CONV_K - 1 and state_ffn_conv.shape[2] == FFN_CONV_K - 1
    past = page_table.shape[1] * cache_k.shape[2]
    alpha = (2 * depth) ** 0.25
    q_scale = (D_HEAD ** -0.5) * LOG2E

    rows_p = _Rows(bp * seq, min(TM_PROMPT, seq), seq)
    rows_s = _Rows(bs, bs, 1)

    n_c = bp + bs
    pad = (-n_c) % 16
    c_all = jnp.concatenate([c_prompt, c_sample, jnp.zeros((pad, d), F32)], axis=0)
    ada = _ada(c_all, w_ada, b_ada)

    def mods(layer, rows, lo, hi):
        return [rows.mod(a) for a in jnp.split(ada[layer, lo:hi], 6, axis=-1)]

    mods_p = [mods(l, rows_p, 0, bp) for l in range(depth)]
    mods_s = [mods(l, rows_s, bp, n_c) for l in range(depth)]

    tables_p = _rope_tables(jnp.arange(seq))
    tables_s = _rope_tables(jnp.tile(past + jnp.arange(dec_seq), bs))

    w_ffn_in_bf = w_ffn_in.astype(BF16)
    w_down_bf = w_down.astype(BF16)
    row2 = lambda a: a.reshape(a.shape[0], 1, a.shape[1])

    def layer(l, rows, x, h, mod, nxt, tables, sample):
        sh1, sc1, g1, sh2, sc2, g2 = mod
        del sh1, sc1
        q = _proj_q(rows, h, w_in, l, tables, qk_width, q_scale)
        k, k_bf = _proj_kv(rows, h, w_in, l, qk_width, qk_width, tables)
        v, v_bf = _proj_kv(rows, h, w_in, l, 2 * qk_width, v_width, None)
        col = 2 * qk_width + v_width
        lam_init = 0.8 - 0.6 * math.exp(-0.3 * l)
        lq, sub_g = lambda_qk[l], attn_subln_g[l].reshape(1, V_HEAD)
        conv_args = (dw_w[l], row2(dw_b)[l], row2(conv_ln_g)[l], row2(conv_ln_b)[l])
        if sample:
            (u,) = _proj_glu(rows, h, w_in, glu_b, l, col, conv_ch, 0)
            attn = _paged_attention(q, k, v, cache_k, cache_v, l, page_table, lq, sub_g, lam_init)
            cv = _conv_step(rows, state_conv[l], u, *conv_args)
            conv_hist = jnp.concatenate([state_conv[l][:, 1:], u[:, None, :]], axis=1)
        else:
            u, u_tail = _proj_glu(rows, h, w_in, glu_b, l, col, conv_ch, HALO_ROWS)
            attn = _prompt_attention(q, k_bf, v_bf, lq, sub_g, bp, seq, lam_init)
            cv = _conv_seq(rows, u, *conv_args)
            conv_hist = u_tail[:, HALO_ROWS - (CONV_K - 1):]
        gates = _proj_gate(rows, h, w_in, l, col + 2 * conv_ch, 2 * d)
        x1, h2 = _mix(rows, cv, gates, attn, x, g1, sc2, sh2, w_pw2[l], row2(b_pw2)[l], w_o[l],
                      row2(ln1_g)[l], row2(ln1_b)[l], alpha)
        ffn_args = (ffn_dw_w[l], row2(ffn_dw_b)[l], w_down_bf[l], row2(ln2_g)[l], row2(ln2_b)[l], alpha, nxt)
        if sample:
            gt, up = _ffn_in(rows, h2, w_ffn_in_bf, l, ffn, F32, 0)
            hist = state_ffn_conv[l]
            x2, h_next = _ffn_down(rows, gt, up, x1, g2, *ffn_args, (hist[:, 0], hist[:, 1]))
            ffn_hist = jnp.concatenate([hist[:, 1:], gt[:, None, :]], axis=1)
        else:
            gt, up, gt_tail = _ffn_in(rows, h2, w_ffn_in_bf, l, ffn, BF16, FFN_HALO_ROWS)
            x2, h_next = _ffn_down(rows, gt, up, x1, g2, *ffn_args, None)
            ffn_hist = gt_tail[:, FFN_HALO_ROWS - (FFN_CONV_K - 1):]
        return x2, h_next, k, v, conv_hist, ffn_hist

    xp = x_prompt.reshape(bp * seq, d)
    xs = x_sample.reshape(bs, d)
    hp = _modulate(rows_p, xp, mods_p[0][1], mods_p[0][0])
    hs = _modulate(rows_s, xs, mods_s[0][1], mods_s[0][0])
    outs_p, outs_s = [], []
    for l in range(depth):
        nxt_p = (mods_p[l + 1][1], mods_p[l + 1][0]) if l + 1 < depth else None
        nxt_s = (mods_s[l + 1][1], mods_s[l + 1][0]) if l + 1 < depth else None
        xp, hp, *rest = layer(l, rows_p, xp, hp, mods_p[l], nxt_p, tables_p, False)
        outs_p.append(rest)
        xs, hs, *rest = layer(l, rows_s, xs, hs, mods_s[l], nxt_s, tables_s, True)
        outs_s.append(rest)

    def stack(outs, i, shape):
        return jnp.stack([o[i].reshape(shape) for o in outs])

    kd = cache_k.shape[3]
    return (xp.reshape(bp, seq, d), xs.reshape(bs, dec_seq, d),
            stack(outs_p, 0, (bp, seq, kd, D_HEAD)), stack(outs_p, 1, (bp, seq, n_heads, V_HEAD)),
            stack(outs_p, 2, (bp, CONV_K - 1, conv_ch)), stack(outs_p, 3, (bp, FFN_CONV_K - 1, ffn)),
            stack(outs_s, 0, (bs, dec_seq, kd, D_HEAD)), stack(outs_s, 1, (bs, dec_seq, n_heads, V_HEAD)),
            stack(outs_s, 2, (bs, CONV_K - 1, conv_ch)), stack(outs_s, 3, (bs, FFN_CONV_K - 1, ffn)))
```

```python
import functools
import math

import jax
import jax.numpy as jnp
from jax import lax
from jax.experimental import pallas as pl
from jax.experimental.pallas import tpu as pltpu

BF16 = jnp.bfloat16
F32 = jnp.float32

D_HEAD = 64
V_HEAD = 2 * D_HEAD
ROT_DIM = D_HEAD // 4
ROPE_THETA = 500000.0
CONV_K = 31
FFN_CONV_K = 3
LN_EPS = 1e-5
LANES = 128
NEG_BIG = -1e30
LOG2E = 1.4426950408889634
VMEM_LIMIT_BYTES = 56 * 1024 * 1024

TM_PROMPT = 1024
TN_PROJ = 1024
TM_FFN_DOWN = 512
TQ_ATTN = 256
TK_ATTN = 512
CONV_ROWS = 64
PAGES_PER_STEP = 16
HEADS_PER_STEP = 8
QK_AHEAD = 2


def _params(*sem):
    return pltpu.CompilerParams(dimension_semantics=sem, vmem_limit_bytes=VMEM_LIMIT_BYTES)


def _silu(z):
    return z * jax.nn.sigmoid(z)


def _layer_norm(y, g, b):
    mu = jnp.mean(y, axis=-1, keepdims=True)
    d = y - mu
    var = jnp.mean(d * d, axis=-1, keepdims=True)
    return d * lax.rsqrt(var + LN_EPS) * g + b


def _dot(a, b):
    return jnp.dot(a, b, preferred_element_type=F32)


def _ada_kernel(c_ref, w_ref, b_ref, o_ref):
    s = _silu(c_ref[...]).astype(BF16)
    o_ref[...] = _dot(s, w_ref[...].astype(BF16)) + b_ref[...]


def _ada(c_all, w_ada, b_ada):
    nl, d, n = w_ada.shape
    mc = c_all.shape[0]
    tn = 1536 if n % 1536 == 0 else n
    return pl.pallas_call(
        _ada_kernel,
        grid=(nl, n // tn),
        in_specs=[pl.BlockSpec((mc, d), lambda l, j: (0, 0)),
                  pl.BlockSpec((None, d, tn), lambda l, j: (l, 0, j)),
                  pl.BlockSpec((None, 1, tn), lambda l, j: (l, 0, j))],
        out_specs=pl.BlockSpec((None, mc, tn), lambda l, j: (l, 0, j)),
        out_shape=jax.ShapeDtypeStruct((nl, mc, n), F32),
        compiler_params=_params("arbitrary", "arbitrary"),
        name="ada",
    )(c_all, w_ada, b_ada.reshape(nl, 1, n))


class _Rows:
    def __init__(self, m, tm, seq):
        self.m, self.tm, self.seq = m, tm, seq
        self.nt = m // tm
        if seq >= tm:
            assert seq % tm == 0
            self.tpb, self.mod_rows = seq // tm, 1
        else:
            assert seq == 1
            self.tpb, self.mod_rows = 1, tm

    def mod(self, a):
        return a.reshape(-1, self.mod_rows, a.shape[-1])

    def mod_spec(self, d, grid_rank):
        if grid_rank == 1:
            return pl.BlockSpec((None, self.mod_rows, d), lambda i: (i // self.tpb, 0, 0))
        return pl.BlockSpec((None, self.mod_rows, d), lambda j, i: (i // self.tpb, 0, 0))


def _modulate_kernel(x_ref, sc_ref, sh_ref, h_ref):
    h_ref[...] = (x_ref[...] * (1.0 + sc_ref[...]) + sh_ref[...]).astype(BF16)


def _modulate(rows, x, sc, sh):
    d = x.shape[1]
    row_spec = pl.BlockSpec((rows.tm, d), lambda i: (i, 0))
    return pl.pallas_call(
        _modulate_kernel,
        grid=(rows.nt,),
        in_specs=[row_spec, rows.mod_spec(d, 1), rows.mod_spec(d, 1)],
        out_specs=row_spec,
        out_shape=jax.ShapeDtypeStruct(x.shape, BF16),
        compiler_params=_params("arbitrary"),
        name="modulate",
    )(x, sc, sh)


def _load_weight(w_ref, wbf_ref):
    @pl.when(pl.program_id(1) == 0)
    def _():
        wbf_ref[...] = w_ref[...].astype(BF16)


def _rope_chunk(y, c, sa, sb):
    return y * c + pltpu.roll(y, LANES - ROT_DIM // 2, 1) * sa + pltpu.roll(y, ROT_DIM // 2, 1) * sb


def _q_kernel(h_ref, w_ref, c_ref, sa_ref, sb_ref, q_ref, wbf_ref, *, scale):
    _load_weight(w_ref, wbf_ref)
    y = _dot(h_ref[...], wbf_ref[...])
    c, sa, sb = c_ref[...], sa_ref[...], sb_ref[...]
    for j in range(y.shape[1] // LANES):
        ls = slice(j * LANES, (j + 1) * LANES)
        q_ref[:, ls] = (_rope_chunk(y[:, ls], c, sa, sb) * scale).astype(BF16)


def _kv_kernel(h_ref, w_ref, *rest, rope):
    if rope:
        c_ref, sa_ref, sb_ref, o_ref, obf_ref, wbf_ref = rest
    else:
        o_ref, obf_ref, wbf_ref = rest
    _load_weight(w_ref, wbf_ref)
    y = _dot(h_ref[...], wbf_ref[...])
    if rope:
        c, sa, sb = c_ref[...], sa_ref[...], sb_ref[...]
        for j in range(y.shape[1] // LANES):
            ls = slice(j * LANES, (j + 1) * LANES)
            r = _rope_chunk(y[:, ls], c, sa, sb)
            o_ref[:, ls] = r
            obf_ref[:, ls] = r.astype(BF16)
    else:
        o_ref[...] = y
        obf_ref[...] = y.astype(BF16)


def _glu_kernel(h_ref, wa_ref, wb_ref, ba_ref, bb_ref, u_ref, *rest, tail):
    if tail:
        tail_ref, wabf_ref, wbbf_ref = rest
    else:
        wabf_ref, wbbf_ref = rest
    _load_weight(wa_ref, wabf_ref)
    _load_weight(wb_ref, wbbf_ref)
    h = h_ref[...]
    a = _dot(h, wabf_ref[...]) + ba_ref[...]
    b = _dot(h, wbbf_ref[...]) + bb_ref[...]
    u = a * jax.nn.sigmoid(b)
    u_ref[...] = u
    if tail:
        tail_ref[...] = u[u.shape[0] - tail:, :]


def _gate_kernel(h_ref, w_ref, o_ref, wbf_ref):
    _load_weight(w_ref, wbf_ref)
    o_ref[...] = jax.nn.sigmoid(_dot(h_ref[...], wbf_ref[...])).astype(BF16)


def _proj_specs(rows, d, tn, col0, layer):
    h_spec = pl.BlockSpec((rows.tm, d), lambda j, i: (i, 0))
    w_spec = pl.BlockSpec((None, d, tn), lambda j, i: (layer, 0, col0 // tn + j))
    o_spec = pl.BlockSpec((rows.tm, tn), lambda j, i: (i, j))
    return h_spec, w_spec, o_spec


def _table_specs(rows, tables):
    nt = tables[0].shape[0] // rows.tm
    return [pl.BlockSpec((rows.tm, LANES), lambda j, i: (i % nt, 0))] * 3


def _proj_q(rows, h, w_in, layer, tables, width, scale):
    d = h.shape[1]
    tn = min(TN_PROJ, width)
    h_spec, w_spec, o_spec = _proj_specs(rows, d, tn, 0, layer)
    return pl.pallas_call(
        functools.partial(_q_kernel, scale=scale),
        grid=(width // tn, rows.nt),
        in_specs=[h_spec, w_spec] + _table_specs(rows, tables),
        out_specs=o_spec,
        out_shape=jax.ShapeDtypeStruct((rows.m, width), BF16),
        scratch_shapes=[pltpu.VMEM((d, tn), BF16)],
        compiler_params=_params("arbitrary", "arbitrary"),
        name="proj_q",
    )(h, w_in, *tables)


def _proj_kv(rows, h, w_in, layer, col0, width, tables):
    d = h.shape[1]
    tn = min(TN_PROJ, width)
    h_spec, w_spec, o_spec = _proj_specs(rows, d, tn, col0, layer)
    rope = tables is not None
    extra = _table_specs(rows, tables) if rope else []
    args = tuple(tables) if rope else ()
    return pl.pallas_call(
        functools.partial(_kv_kernel, rope=rope),
        grid=(width // tn, rows.nt),
        in_specs=[h_spec, w_spec] + extra,
        out_specs=[o_spec, o_spec],
        out_shape=[jax.ShapeDtypeStruct((rows.m, width), F32),
                   jax.ShapeDtypeStruct((rows.m, width), BF16)],
        scratch_shapes=[pltpu.VMEM((d, tn), BF16)],
        compiler_params=_params("arbitrary", "arbitrary"),
        name="proj_k" if rope else "proj_v",
    )(h, w_in, *args)


def _proj_glu(rows, h, w_in, glu_b, layer, col0, width, tail):
    d = h.shape[1]
    tn = min(TN_PROJ, width)
    nl = glu_b.shape[0]
    h_spec, wa_spec, o_spec = _proj_specs(rows, d, tn, col0, layer)
    _, wb_spec, _ = _proj_specs(rows, d, tn, col0 + width, layer)
    ba_spec = pl.BlockSpec((None, 1, tn), lambda j, i: (layer, 0, j))
    bb_spec = pl.BlockSpec((None, 1, tn), lambda j, i: (layer, 0, width // tn + j))
    out_specs = [o_spec]
    out_shape = [jax.ShapeDtypeStruct((rows.m, width), F32)]
    if tail:
        out_specs.append(pl.BlockSpec((None, tail, tn), lambda j, i: (i // rows.tpb, 0, j)))
        out_shape.append(jax.ShapeDtypeStruct((rows.m // rows.seq, tail, width), F32))
    gb = glu_b.reshape(nl, 1, 2 * width)
    return pl.pallas_call(
        functools.partial(_glu_kernel, tail=tail),
        grid=(width // tn, rows.nt),
        in_specs=[h_spec, wa_spec, wb_spec, ba_spec, bb_spec],
        out_specs=out_specs,
        out_shape=out_shape,
        scratch_shapes=[pltpu.VMEM((d, tn), BF16), pltpu.VMEM((d, tn), BF16)],
        compiler_params=_params("arbitrary", "arbitrary"),
        name="proj_glu",
    )(h, w_in, w_in, gb, gb)


def _proj_gate(rows, h, w_in, layer, col0, width):
    d = h.shape[1]
    tn = min(TN_PROJ, width)
    h_spec, w_spec, o_spec = _proj_specs(rows, d, tn, col0, layer)
    return pl.pallas_call(
        _gate_kernel,
        grid=(width // tn, rows.nt),
        in_specs=[h_spec, w_spec],
        out_specs=o_spec,
        out_shape=jax.ShapeDtypeStruct((rows.m, width), BF16),
        scratch_shapes=[pltpu.VMEM((d, tn), BF16)],
        compiler_params=_params("arbitrary", "arbitrary"),
        name="proj_gate",
    )(h, w_in)


def _lambda(lq_ref, lam_init):
    lq = lq_ref[...]
    a = jnp.sum(lq[0:1] * lq[1:2], axis=1, keepdims=True)
    b = jnp.sum(lq[2:3] * lq[3:4], axis=1, keepdims=True)
    return jnp.exp(a) - jnp.exp(b) + lam_init


def _sub_norm(o, g, lam_init):
    ms = jnp.mean(o * o, axis=-1, keepdims=True)
    return o * lax.rsqrt(ms + LN_EPS) * g * (1.0 - lam_init)


def _attn_kernel(lq_ref, g_ref, q_ref, k_ref, vt_ref, o_ref, acc_ref, m_ref, l_ref, *, tq, tk, hps, lam_init):
    i = pl.program_id(2)
    q2 = []
    for hh in range(hps):
        q = q_ref[:, hh * V_HEAD:(hh + 1) * V_HEAD]
        lane = lax.broadcasted_iota(jnp.int32, q.shape, 1)
        zero = jnp.zeros_like(q)
        q2.append(jnp.concatenate([jnp.where(lane < D_HEAD, q, zero), jnp.where(lane >= D_HEAD, q, zero)],
                                  axis=0))
    m_ref[...] = jnp.full(m_ref.shape, NEG_BIG, F32)
    l_ref[...] = jnp.zeros(l_ref.shape, F32)
    acc_ref[...] = jnp.zeros(acc_ref.shape, F32)

    def step(j, masked, part=None):
        pi, pn = part if part else (0, 1)
        nk = tk // pn
        k0 = pl.multiple_of(j * tk + pi * nk, nk)

        def qk(hh):
            kt = k_ref[pl.ds(k0, nk), hh * V_HEAD:(hh + 1) * V_HEAD]
            return lax.dot_general(kt, q2[hh], (((1,), (1,)), ((), ())), preferred_element_type=F32)

        scores = [qk(hh) for hh in range(min(QK_AHEAD, hps))]
        for hh in range(hps):
            if hh + QK_AHEAD < hps:
                scores.append(qk(hh + QK_AHEAD))
            s = scores[hh]
            if masked:
                kpos = k0 + lax.broadcasted_iota(jnp.int32, s.shape, 0)
                qpos = i * tq + (lax.broadcasted_iota(jnp.int32, s.shape, 1) & (tq - 1))
                s = jnp.where(kpos <= qpos, s, NEG_BIG)
            m_prev = m_ref[hh]
            m_new = jnp.maximum(m_prev, jnp.max(s, axis=0, keepdims=True))
            alpha = jnp.exp2(m_prev - m_new)
            p = jnp.exp2(s - m_new)
            vt = vt_ref[hh, j, :, pi * nk:(pi + 1) * nk]
            acc_ref[hh] = acc_ref[hh] * alpha + _dot(vt, p.astype(BF16))
            m_ref[hh] = m_new

    n_full = (i * tq) // tk

    def full_step(j, carry):
        step(j, False)
        return carry

    lax.fori_loop(0, n_full, full_step, 0)
    parts = tk // tq
    diag = i % parts
    for pi in range(parts):
        if pi < parts - 1:
            @pl.when(pi < diag)
            def _(pi=pi):
                step(n_full, False, (pi, parts))

        @pl.when(pi == diag)
        def _(pi=pi):
            step(n_full, True, (pi, parts))

    lam = _lambda(lq_ref, lam_init)
    for hh in range(hps):
        o2 = acc_ref[hh, 0:V_HEAD] * (1.0 / acc_ref[hh, V_HEAD:V_HEAD + 1])
        ot = o2[:, :tq] - lam * o2[:, tq:]
        o_ref[:, hh * V_HEAD:(hh + 1) * V_HEAD] = _sub_norm(ot.T, g_ref[...], lam_init).astype(BF16)


def _prompt_attention(q, k, v, lq, g, batch, seq, lam_init):
    width = q.shape[1]
    nh = width // V_HEAD
    tq = min(TQ_ATTN, seq)
    tk = min(TK_ATTN, seq)
    hps = math.gcd(HEADS_PER_STEP, nh)
    assert tq & (tq - 1) == 0 and tk % tq == 0 and seq % tk == 0
    q3 = q.reshape(batch, seq, width)
    k3 = k.reshape(batch, seq, width)
    vt = v.reshape(batch, seq // tk, tk, nh, V_HEAD).transpose(0, 3, 1, 4, 2)
    vt = jnp.concatenate([vt, jnp.ones(vt.shape[:3] + (16, tk), vt.dtype)], axis=3)
    gw = hps * V_HEAD
    out = pl.pallas_call(
        functools.partial(_attn_kernel, tq=tq, tk=tk, hps=hps, lam_init=lam_init),
        grid=(batch, nh // hps, seq // tq),
        in_specs=[pl.BlockSpec((4, D_HEAD), lambda b, h, i: (0, 0)),
                  pl.BlockSpec((1, V_HEAD), lambda b, h, i: (0, 0)),
                  pl.BlockSpec((None, tq, gw), lambda b, h, i: (b, i, h)),
                  pl.BlockSpec((None, seq, gw), lambda b, h, i: (b, 0, h)),
                  pl.BlockSpec((None, hps, seq // tk, V_HEAD + 16, tk), lambda b, h, i: (b, h, 0, 0, 0))],
        out_specs=pl.BlockSpec((None, tq, gw), lambda b, h, i: (b, i, h)),
        out_shape=jax.ShapeDtypeStruct((batch, seq, width), BF16),
        scratch_shapes=[pltpu.VMEM((hps, V_HEAD + 16, 2 * tq), F32),
                        pltpu.VMEM((hps, 1, 2 * tq), F32),
                        pltpu.VMEM((hps, 1, 2 * tq), F32)],
        compiler_params=_params("arbitrary", "arbitrary", "arbitrary"),
        name="prompt_attention",
    )(lq, g, q3, k3, vt)
    return out.reshape(batch * seq, width)


def _paged_kernel(pt_ref, lq_ref, g_ref, qcol_ref, q_ref, kn_ref, vn_ref, *rest, ppg, nj, lam_init):
    del pt_ref
    k_refs, v_refs = rest[:ppg], rest[ppg:2 * ppg]
    o_ref, s_ref, d_ref, acc_ref = rest[2 * ppg:]
    ph, j = pl.program_id(1), pl.program_id(2)
    ncomp, dh, page = k_refs[0].shape
    nh, dv = v_refs[0].shape[1], v_refs[0].shape[2]
    n_past = nj * ppg
    assert nh & (nh - 1) == 0

    @pl.when(ph == 0)
    def _():
        qcol = jnp.broadcast_to(qcol_ref[...], (ncomp * dh, page)).reshape(ncomp, dh, page)
        for gi in range(ppg):
            pg = j * ppg + gi
            s_ref[pl.ds(pl.multiple_of(pg * ncomp, ncomp), ncomp), :] = jnp.sum(k_refs[gi][...] * qcol, axis=1)

    @pl.when((ph == 0) & (j == nj - 1))
    def _():
        s_self = jnp.sum(q_ref[...] * kn_ref[...], axis=1, keepdims=True)
        lane = lax.broadcasted_iota(jnp.int32, (ncomp, page), 1)
        s_ref[n_past * ncomp:(n_past + 1) * ncomp, :] = jnp.where(lane == 0, s_self, NEG_BIG)
        s = s_ref[...].reshape(n_past + 1, ncomp, page)
        m = jnp.max(jnp.max(s, axis=0), axis=1, keepdims=True)
        p = jnp.exp2(s - m[None])
        inv = 1.0 / jnp.sum(jnp.sum(p, axis=0), axis=1, keepdims=True)
        s_ref[...] = (p * inv[None]).reshape((n_past + 1) * ncomp, page)
        lam = _lambda(lq_ref, lam_init)
        rows = (n_past + 1) * nh
        d_ref[...] = s_ref[pl.ds(0, rows, stride=2), :] - lam * s_ref[pl.ds(1, rows, stride=2), :]
        acc_ref[...] = jnp.zeros(acc_ref.shape, F32)

    @pl.when(ph == 1)
    def _():
        src_row = lax.broadcasted_iota(jnp.int32, (page, page * nh), 0)
        dst_row = lax.broadcasted_iota(jnp.int32, (page, page * nh), 1) >> (nh.bit_length() - 1)
        spread = jnp.where(src_row == dst_row, 1.0, 0.0).astype(BF16)
        head = lax.broadcasted_iota(jnp.int32, (nh, page * nh), 0)
        col_head = lax.broadcasted_iota(jnp.int32, (nh, page * nh), 1) & (nh - 1)
        acc = acc_ref[...]
        for gi in range(ppg):
            pg = j * ppg + gi
            d = d_ref[pl.ds(pl.multiple_of(pg * nh, nh), nh), :].astype(BF16)
            dexp = jnp.where(head == col_head, _dot(d, spread), 0.0).astype(BF16)
            acc = acc + _dot(dexp, v_refs[gi][...].reshape(page * nh, dv).astype(BF16))
        acc_ref[...] = acc

    @pl.when((ph == 1) & (j == nj - 1))
    def _():
        d_self = d_ref[n_past * nh:(n_past + 1) * nh, 0:1]
        o = acc_ref[...] + d_self * vn_ref[...]
        o_ref[...] = _sub_norm(o, g_ref[...], lam_init)


def _paged_attention(q, k_new, v_new, cache_k, cache_v, layer, page_table, lq, g, lam_init):
    bs, width = q.shape
    n_pages = page_table.shape[1]
    _, _, page, ncomp, dh = cache_k.shape
    nh, dv = cache_v.shape[3], cache_v.shape[4]
    ppg = math.gcd(PAGES_PER_STEP, n_pages)
    nj = n_pages // ppg
    cache_kt = jnp.transpose(cache_k, (0, 1, 3, 4, 2))

    def k_map(gi):
        return lambda b, ph, j, pt: (layer, pt[b * n_pages + jnp.where(ph == 0, j, nj - 1) * ppg + gi], 0, 0, 0)

    def v_map(gi):
        return lambda b, ph, j, pt: (layer, pt[b * n_pages + jnp.where(ph == 0, 0, j) * ppg + gi], 0, 0, 0)

    def row_spec(r, c):
        return pl.BlockSpec((None, r, c), lambda b, ph, j, pt: (b, 0, 0))

    in_specs = [pl.BlockSpec((4, D_HEAD), lambda b, ph, j, pt: (0, 0)),
                pl.BlockSpec((1, V_HEAD), lambda b, ph, j, pt: (0, 0)),
                row_spec(ncomp * dh, 1), row_spec(ncomp, dh), row_spec(ncomp, dh), row_spec(nh, dv)]
    in_specs += [pl.BlockSpec((None, None, ncomp, dh, page), k_map(gi)) for gi in range(ppg)]
    in_specs += [pl.BlockSpec((None, None, page, nh, dv), v_map(gi)) for gi in range(ppg)]
    qf = q.astype(F32)
    out = pl.pallas_call(
        functools.partial(_paged_kernel, ppg=ppg, nj=nj, lam_init=lam_init),
        grid_spec=pltpu.PrefetchScalarGridSpec(
            num_scalar_prefetch=1,
            grid=(bs, 2, nj),
            in_specs=in_specs,
            out_specs=row_spec(nh, dv),
            scratch_shapes=[pltpu.VMEM(((n_pages + 1) * ncomp, page), F32),
                            pltpu.VMEM(((n_pages + 1) * nh, page), F32),
                            pltpu.VMEM((nh, dv), F32)]),
        out_shape=jax.ShapeDtypeStruct((bs, nh, dv), F32),
        compiler_params=_params("arbitrary", "arbitrary", "arbitrary"),
        name="paged_attention",
    )(page_table.reshape(-1), lq, g, qf.reshape(bs, ncomp * dh, 1), qf.reshape(bs, ncomp, dh),
      k_new.reshape(bs, ncomp, dh), v_new.reshape(bs, nh, dv), *([cache_kt] * ppg), *([cache_v] * ppg))
    return out.reshape(bs, width)


HALO_ROWS = 32


def _conv_seq_kernel(u_ref, halo_ref, w_ref, b_ref, g_ref, beta_ref, cv_ref, ext_ref, y_ref, wb_ref, *, tpb):
    tm, d = u_ref.shape
    first = (pl.program_id(0) % tpb) == 0
    ext_ref[0:HALO_ROWS, :] = jnp.where(first, 0.0, halo_ref[...])
    ext_ref[HALO_ROWS:, :] = u_ref[...]
    for k in range(CONV_K):
        wb_ref[k] = jnp.broadcast_to(w_ref[k:k + 1, :], (8, d))
    lead = HALO_ROWS - (CONV_K - 1)
    rc = min(CONV_ROWS, tm)

    def chunk(r, carry):
        r0 = pl.multiple_of(r * rc, rc)
        for j in range(d // LANES):
            ls = slice(j * LANES, (j + 1) * LANES)
            blk = ext_ref[pl.ds(r0, rc + HALO_ROWS), ls]
            acc = jnp.zeros((rc // 8, 8, LANES), F32)
            for s in range(8):
                taps = [k for k in range(CONV_K) if (lead + k) % 8 == s]
                if not taps:
                    continue
                shifted = pltpu.roll(blk, rc + HALO_ROWS - s, 0) if s else blk
                for k in taps:
                    a = lead + k - s
                    acc = acc + wb_ref[k, :, ls][None] * shifted[a:a + rc].reshape(rc // 8, 8, LANES)
            y_ref[pl.ds(r0, rc), ls] = acc.reshape(rc, LANES) + b_ref[:, ls]
        return carry

    lax.fori_loop(0, tm // rc, chunk, 0)
    cv_ref[...] = _silu(_layer_norm(y_ref[...], g_ref[...], beta_ref[...])).astype(BF16)


def _conv_seq(rows, u, w, b, g, beta):
    d = u.shape[1]
    tm = rows.tm
    const = lambda i: (0, 0)
    return pl.pallas_call(
        functools.partial(_conv_seq_kernel, tpb=rows.tpb),
        grid=(rows.nt,),
        in_specs=[pl.BlockSpec((tm, d), lambda i: (i, 0)),
                  pl.BlockSpec((HALO_ROWS, d), lambda i: (jnp.maximum(i * (tm // HALO_ROWS) - 1, 0), 0)),
                  pl.BlockSpec((CONV_K, d), const),
                  pl.BlockSpec((1, d), const), pl.BlockSpec((1, d), const), pl.BlockSpec((1, d), const)],
        out_specs=pl.BlockSpec((tm, d), lambda i: (i, 0)),
        out_shape=jax.ShapeDtypeStruct(u.shape, BF16),
        scratch_shapes=[pltpu.VMEM((tm + HALO_ROWS, d), F32), pltpu.VMEM((tm, d), F32),
                        pltpu.VMEM((CONV_K, 8, d), F32)],
        compiler_params=_params("arbitrary"),
        name="conv_seq",
    )(u, u, w, b, g, beta)


def _conv_step_kernel(st_ref, u_ref, w_ref, b_ref, g_ref, beta_ref, cv_ref, acc_ref):
    k = pl.program_id(1)
    wk = w_ref[pl.ds(k, 1), :]

    @pl.when(k == 0)
    def _():
        acc_ref[...] = jnp.zeros(acc_ref.shape, F32)

    @pl.when(k < CONV_K - 1)
    def _():
        acc_ref[...] += st_ref[...] * wk

    @pl.when(k == CONV_K - 1)
    def _():
        y = acc_ref[...] + u_ref[...] * wk + b_ref[...]
        cv_ref[...] = _silu(_layer_norm(y, g_ref[...], beta_ref[...])).astype(BF16)


def _conv_step(rows, state, u, w, b, g, beta):
    d = u.shape[1]
    tm = rows.tm
    const = lambda i, k: (0, 0)
    row_spec = pl.BlockSpec((tm, d), lambda i, k: (i, 0))
    return pl.pallas_call(
        _conv_step_kernel,
        grid=(rows.nt, CONV_K),
        in_specs=[pl.BlockSpec((tm, d), lambda i, k: (i, jnp.minimum(k, CONV_K - 2))),
                  row_spec,
                  pl.BlockSpec((CONV_K, d), const),
                  pl.BlockSpec((1, d), const), pl.BlockSpec((1, d), const), pl.BlockSpec((1, d), const)],
        out_specs=row_spec,
        out_shape=jax.ShapeDtypeStruct(u.shape, BF16),
        scratch_shapes=[pltpu.VMEM((tm, d), F32)],
        compiler_params=_params("arbitrary", "arbitrary"),
        name="conv_step",
    )(state.reshape(state.shape[0], (CONV_K - 1) * d), u, w, b, g, beta)


def _mix_kernel(cv_ref, ga_ref, gb_ref, at_ref, x_ref, g1_ref, sc2_ref, sh2_ref, wp_ref, bp_ref, wo_ref,
                lg_ref, lb_ref, x1_ref, h2_ref, wpbf_ref, wobf_ref, *, alpha):
    @pl.when(pl.program_id(0) == 0)
    def _():
        wpbf_ref[...] = wp_ref[...].astype(BF16)
        wobf_ref[...] = wo_ref[...].astype(BF16)

    conv_out = _dot(cv_ref[...], wpbf_ref[...]) + bp_ref[...]
    merged = ga_ref[...].astype(F32) * conv_out + gb_ref[...].astype(F32) * at_ref[...].astype(F32)
    mix = _dot(merged.astype(BF16), wobf_ref[...])
    x1 = _layer_norm(alpha * x_ref[...] + g1_ref[...] * mix, lg_ref[...], lb_ref[...])
    x1_ref[...] = x1
    h2_ref[...] = (x1 * (1.0 + sc2_ref[...]) + sh2_ref[...]).astype(BF16)


def _mix(rows, cv, gates, attn, x, g1, sc2, sh2, w_pw2, b_pw2, w_o, ln_g, ln_b, alpha):
    d = x.shape[1]
    tm = rows.tm
    row_spec = pl.BlockSpec((tm, d), lambda i: (i, 0))
    const = lambda i: (0, 0)
    vec = pl.BlockSpec((1, d), const)
    mat = pl.BlockSpec((d, d), const)
    mod = rows.mod_spec(d, 1)
    return pl.pallas_call(
        functools.partial(_mix_kernel, alpha=alpha),
        grid=(rows.nt,),
        in_specs=[row_spec, row_spec, pl.BlockSpec((tm, d), lambda i: (i, 1)), row_spec, row_spec,
                  mod, mod, mod, mat, vec, mat, vec, vec],
        out_specs=[row_spec, row_spec],
        out_shape=[jax.ShapeDtypeStruct(x.shape, F32), jax.ShapeDtypeStruct(x.shape, BF16)],
        scratch_shapes=[pltpu.VMEM((d, d), BF16), pltpu.VMEM((d, d), BF16)],
        compiler_params=_params("arbitrary"),
        name="mix",
    )(cv, gates, gates, attn, x, g1, sc2, sh2, w_pw2, b_pw2, w_o, ln_g, ln_b)


FFN_HALO_ROWS = 8


def _ffn_in_kernel(h_ref, wg_ref, wu_ref, gt_ref, up_ref, *rest, tail):
    h = h_ref[...]
    gt = _dot(h, wg_ref[...])
    gt_ref[...] = gt.astype(gt_ref.dtype)
    up_ref[...] = _dot(h, wu_ref[...]).astype(BF16)
    if tail:
        rest[0][...] = gt[gt.shape[0] - tail:, :]


def _ffn_in(rows, h, w_ffn_in_bf, layer, ffn, gt_dtype, tail):
    d = h.shape[1]
    tm = rows.tm
    tn = ffn // 2 if (ffn // 2) % LANES == 0 else ffn
    nj = ffn // tn
    o_spec = pl.BlockSpec((tm, tn), lambda j, i: (i, j))
    out_specs = [o_spec, o_spec]
    out_shape = [jax.ShapeDtypeStruct((rows.m, ffn), gt_dtype), jax.ShapeDtypeStruct((rows.m, ffn), BF16)]
    if tail:
        out_specs.append(pl.BlockSpec((None, tail, tn), lambda j, i: (i // rows.tpb, 0, j)))
        out_shape.append(jax.ShapeDtypeStruct((rows.m // rows.seq, tail, ffn), F32))
    return pl.pallas_call(
        functools.partial(_ffn_in_kernel, tail=tail),
        grid=(nj, rows.nt),
        in_specs=[pl.BlockSpec((tm, d), lambda j, i: (i, 0)),
                  pl.BlockSpec((None, d, tn), lambda j, i: (layer, 0, j)),
                  pl.BlockSpec((None, d, tn), lambda j, i: (layer, 0, nj + j))],
        out_specs=out_specs,
        out_shape=out_shape,
        compiler_params=_params("arbitrary", "arbitrary"),
        name="ffn_in",
    )(h, w_ffn_in_bf, w_ffn_in_bf)


def _ffn_tail(act, x_ref, g2_ref, wd_ref, lg_ref, lb_ref, x2_ref, nxt, alpha):
    f = _dot(act.astype(BF16), wd_ref[...])
    x2 = _layer_norm(alpha * x_ref[...] + g2_ref[...] * f, lg_ref[...], lb_ref[...])
    x2_ref[...] = x2
    if nxt:
        sc_ref, sh_ref, hn_ref = nxt
        hn_ref[...] = (x2 * (1.0 + sc_ref[...]) + sh_ref[...]).astype(BF16)


def _ffn_down_seq_kernel(gt_ref, halo_ref, up_ref, x_ref, g2_ref, cw_ref, cb_ref, wd_ref, lg_ref, lb_ref,
                         *rest, tpb, alpha, has_next):
    if has_next:
        sc_ref, sh_ref, x2_ref, hn_ref, ext_ref = rest
        nxt = (sc_ref, sh_ref, hn_ref)
    else:
        x2_ref, ext_ref = rest
        nxt = None
    tm = gt_ref.shape[0]
    first = (pl.program_id(0) % tpb) == 0
    ext_ref[0:FFN_HALO_ROWS, :] = jnp.where(first, 0.0, halo_ref[...].astype(F32))
    ext_ref[FFN_HALO_ROWS:, :] = gt_ref[...].astype(F32)
    lead = FFN_HALO_ROWS - (FFN_CONV_K - 1)
    ext = ext_ref[...]
    conv = cb_ref[...]
    for k in range(FFN_CONV_K):
        off = lead + k
        shifted = pltpu.roll(ext, tm + FFN_HALO_ROWS - off, 0)[0:tm] if off % 8 else ext[off:off + tm]
        conv = conv + cw_ref[k:k + 1, :] * shifted
    _ffn_tail(_silu(conv) * up_ref[...].astype(F32), x_ref, g2_ref, wd_ref, lg_ref, lb_ref, x2_ref, nxt, alpha)


def _ffn_down_step_kernel(gt_ref, p2_ref, p1_ref, up_ref, x_ref, g2_ref, cw_ref, cb_ref, wd_ref, lg_ref, lb_ref,
                          *rest, alpha, has_next):
    if has_next:
        sc_ref, sh_ref, x2_ref, hn_ref = rest
        nxt = (sc_ref, sh_ref, hn_ref)
    else:
        (x2_ref,) = rest
        nxt = None
    conv = (cb_ref[...] + cw_ref[0:1, :] * p2_ref[...] + cw_ref[1:2, :] * p1_ref[...]
            + cw_ref[2:3, :] * gt_ref[...].astype(F32))
    _ffn_tail(_silu(conv) * up_ref[...].astype(F32), x_ref, g2_ref, wd_ref, lg_ref, lb_ref, x2_ref, nxt, alpha)


def _ffn_down(rows, gt, up, x, g2, conv_w, conv_b, w_down_bf, ln_g, ln_b, alpha, nxt, prev):
    m, d = x.shape
    ffn = gt.shape[1]
    tm = min(TM_FFN_DOWN, rows.tm)
    sub = rows.tm // tm
    tpb = rows.tpb * sub if rows.mod_rows == 1 else 1
    row_d = pl.BlockSpec((tm, d), lambda i: (i, 0))
    row_f = pl.BlockSpec((tm, ffn), lambda i: (i, 0))
    const = lambda i: (0, 0)
    vec_d = pl.BlockSpec((1, d), const)
    if rows.mod_rows == 1:
        mod = pl.BlockSpec((None, 1, d), lambda i: (i // tpb, 0, 0))
        mods = lambda a: a
    else:
        mod = pl.BlockSpec((tm, d), lambda i: (i, 0))
        mods = lambda a: a.reshape(m, d)
    common_in = [row_f, row_d, mod, pl.BlockSpec((FFN_CONV_K, ffn), const), pl.BlockSpec((1, ffn), const),
                 pl.BlockSpec((ffn, d), const), vec_d, vec_d]
    common_args = [up, x, mods(g2), conv_w, conv_b, w_down_bf, ln_g, ln_b]
    out_specs = [row_d]
    out_shape = [jax.ShapeDtypeStruct((m, d), F32)]
    if nxt:
        common_in += [mod, mod]
        common_args += [mods(nxt[0]), mods(nxt[1])]
        out_specs.append(row_d)
        out_shape.append(jax.ShapeDtypeStruct((m, d), BF16))
    if prev is None:
        body = functools.partial(_ffn_down_seq_kernel, tpb=tpb, alpha=alpha, has_next=bool(nxt))
        lead_in = [row_f, pl.BlockSpec((FFN_HALO_ROWS, ffn),
                                       lambda i: (jnp.maximum(i * (tm // FFN_HALO_ROWS) - 1, 0), 0))]
        lead_args = [gt, gt]
        scratch = [pltpu.VMEM((tm + FFN_HALO_ROWS, ffn), F32)]
    else:
        body = functools.partial(_ffn_down_step_kernel, alpha=alpha, has_next=bool(nxt))
        lead_in = [row_f, row_f, row_f]
        lead_args = [gt, prev[0], prev[1]]
        scratch = []
    res = pl.pallas_call(
        body,
        grid=(m // tm,),
        in_specs=lead_in + common_in,
        out_specs=out_specs,
        out_shape=out_shape,
        scratch_shapes=scratch,
        compiler_params=_params("arbitrary"),
        name="ffn_down",
    )(*lead_args, *common_args)
    return (res[0], res[1]) if nxt else (res[0], None)


def _rope_tables(pos):
    half = ROT_DIM // 2
    inv = ROPE_THETA ** (-jnp.arange(half, dtype=F32) / half)
    ang = pos.astype(F32)[:, None] * inv[None, :]
    cos, sin = jnp.cos(ang), jnp.sin(ang)
    t = pos.shape[0]
    zeros = jnp.zeros((t, half), F32)
    rest = D_HEAD - ROT_DIM
    c = jnp.concatenate([cos, cos, jnp.ones((t, rest), F32)], axis=1)
    sa = jnp.concatenate([-sin, zeros, jnp.zeros((t, rest), F32)], axis=1)
    sb = jnp.concatenate([zeros, sin, jnp.zeros((t, rest), F32)], axis=1)
    reps = LANES // D_HEAD
    return tuple(jnp.tile(a, (1, reps)) for a in (c, sa, sb))


def kernel(x_prompt, x_sample, cache_k, cache_v, state_conv, state_ffn_conv, page_table, c_prompt, c_sample,
           w_ada, b_ada, w_in, lambda_qk, attn_subln_g, glu_b, dw_w, dw_b, conv_ln_g, conv_ln_b, w_pw2, b_pw2,
           w_o, ln1_g, ln1_b, w_ffn_in, ffn_dw_w, ffn_dw_b, w_down, ln2_g, ln2_b):
    depth, d, _ = w_in.shape
    bp, seq, _ = x_prompt.shape
    bs, dec_seq, _ = x_sample.shape
    assert dec_seq == 1
    n_heads = cache_v.shape[3]
    qk_width = cache_k.shape[3] * cache_k.shape[4]
    v_width = n_heads * cache_v.shape[4]
    conv_ch = state_conv.shape[-1]
    ffn = state_ffn_conv.shape[-1]
    assert cache_k.shape[4] == D_HEAD and cache_v.shape[4] == V_HEAD and qk_width == v_width
    assert state_conv.shape[2] == CONV_K - 1 and state_ffn_conv.shape[2] == FFN_CONV_K - 1
    past = page_table.shape[1] * cache_k.shape[2]
    alpha = (2 * depth) ** 0.25
    q_scale = (D_HEAD ** -0.5) * LOG2E

    rows_p = _Rows(bp * seq, min(TM_PROMPT, seq), seq)
    rows_s = _Rows(bs, bs, 1)

    n_c = bp + bs
    pad = (-n_c) % 16
    c_all = jnp.concatenate([c_prompt, c_sample, jnp.zeros((pad, d), F32)], axis=0)
    ada = _ada(c_all, w_ada, b_ada)

    def mods(layer, rows, lo, hi):
        return [rows.mod(a) for a in jnp.split(ada[layer, lo:hi], 6, axis=-1)]

    mods_p = [mods(l, rows_p, 0, bp) for l in range(depth)]
    mods_s = [mods(l, rows_s, bp, n_c) for l in range(depth)]

    tables_p = _rope_tables(jnp.arange(seq))
    tables_s = _rope_tables(jnp.tile(past + jnp.arange(dec_seq), bs))

    w_ffn_in_bf = w_ffn_in.astype(BF16)
    w_down_bf = w_down.astype(BF16)
    row2 = lambda a: a.reshape(a.shape[0], 1, a.shape[1])

    def layer(l, rows, x, h, mod, nxt, tables, sample):
        sh1, sc1, g1, sh2, sc2, g2 = mod
        del sh1, sc1
        q = _proj_q(rows, h, w_in, l, tables, qk_width, q_scale)
        k, k_bf = _proj_kv(rows, h, w_in, l, qk_width, qk_width, tables)
        v, v_bf = _proj_kv(rows, h, w_in, l, 2 * qk_width, v_width, None)
        col = 2 * qk_width + v_width
        lam_init = 0.8 - 0.6 * math.exp(-0.3 * l)
        lq, sub_g = lambda_qk[l], attn_subln_g[l].reshape(1, V_HEAD)
        conv_args = (dw_w[l], row2(dw_b)[l], row2(conv_ln_g)[l], row2(conv_ln_b)[l])
        if sample:
            (u,) = _proj_glu(rows, h, w_in, glu_b, l, col, conv_ch, 0)
            attn = _paged_attention(q, k, v, cache_k, cache_v, l, page_table, lq, sub_g, lam_init)
            cv = _conv_step(rows, state_conv[l], u, *conv_args)
            conv_hist = jnp.concatenate([state_conv[l][:, 1:], u[:, None, :]], axis=1)
        else:
            u, u_tail = _proj_glu(rows, h, w_in, glu_b, l, col, conv_ch, HALO_ROWS)
            attn = _prompt_attention(q, k_bf, v_bf, lq, sub_g, bp, seq, lam_init)
            cv = _conv_seq(rows, u, *conv_args)
            conv_hist = u_tail[:, HALO_ROWS - (CONV_K - 1):]
        gates = _proj_gate(rows, h, w_in, l, col + 2 * conv_ch, 2 * d)
        x1, h2 = _mix(rows, cv, gates, attn, x, g1, sc2, sh2, w_pw2[l], row2(b_pw2)[l], w_o[l],
                      row2(ln1_g)[l], row2(ln1_b)[l], alpha)
        ffn_args = (ffn_dw_w[l], row2(ffn_dw_b)[l], w_down_bf[l], row2(ln2_g)[l], row2(ln2_b)[l], alpha, nxt)
        if sample:
            gt, up = _ffn_in(rows, h2, w_ffn_in_bf, l, ffn, F32, 0)
            hist = state_ffn_conv[l]
            x2, h_next = _ffn_down(rows, gt, up, x1, g2, *ffn_args, (hist[:, 0], hist[:, 1]))
            ffn_hist = jnp.concatenate([hist[:, 1:], gt[:, None, :]], axis=1)
        else:
            gt, up, gt_tail = _ffn_in(rows, h2, w_ffn_in_bf, l, ffn, BF16, FFN_HALO_ROWS)
            x2, h_next = _ffn_down(rows, gt, up, x1, g2, *ffn_args, None)
            ffn_hist = gt_tail[:, FFN_HALO_ROWS - (FFN_CONV_K - 1):]
        return x2, h_next, k, v, conv_hist, ffn_hist

    xp = x_prompt.reshape(bp * seq, d)
    xs = x_sample.reshape(bs, d)
    hp = _modulate(rows_p, xp, mods_p[0][1], mods_p[0][0])
    hs = _modulate(rows_s, xs, mods_s[0][1], mods_s[0][0])
    outs_p, outs_s = [], []
    for l in range(depth):
        nxt_p = (mods_p[l + 1][1], mods_p[l + 1][0]) if l + 1 < depth else None
        nxt_s = (mods_s[l + 1][1], mods_s[l + 1][0]) if l + 1 < depth else None
        xp, hp, *rest = layer(l, rows_p, xp, hp, mods_p[l], nxt_p, tables_p, False)
        outs_p.append(rest)
        xs, hs, *rest = layer(l, rows_s, xs, hs, mods_s[l], nxt_s, tables_s, True)
        outs_s.append(rest)

    def stack(outs, i, shape):
        return jnp.stack([o[i].reshape(shape) for o in outs])

    kd = cache_k.shape[3]
    return (xp.reshape(bp, seq, d), xs.reshape(bs, dec_seq, d),
            stack(outs_p, 0, (bp, seq, kd, D_HEAD)), stack(outs_p, 1, (bp, seq, n_heads, V_HEAD)),
            stack(outs_p, 2, (bp, CONV_K - 1, conv_ch)), stack(outs_p, 3, (bp, FFN_CONV_K - 1, ffn)),
            stack(outs_s, 0, (bs, dec_seq, kd, D_HEAD)), stack(outs_s, 1, (bs, dec_seq, n_heads, V_HEAD)),
            stack(outs_s, 2, (bs, CONV_K - 1, conv_ch)), stack(outs_s, 3, (bs, FFN_CONV_K - 1, ffn)))
```
